```python
import math
import jax, jax.numpy as jnp
from jax import lax
import numpy as np

D_MODEL = 1024
BATCH = 16
SEQ = 2048
DEPTH = 2
DEC_BATCH = 16
DEC_SEQ = 4096
PAST_LEN = 128

DN_ALPHA = (2 * DEPTH) ** 0.25
DN_BETA = (8 * DEPTH) ** -0.25
LN_EPS = 1e-5
RMS_EPS = 1e-6

MLA_HEADS = 8
MLA_Q_RANK = 768
MLA_KV_RANK = 256
MLA_NOPE = 64
MLA_ROPE = 32
MLA_V = 64
ROPE_BASE = 10000.0
Q_BLOCK = 128

GLA_HEADS = 4
GLA_DK = 64
GLA_DV = 128
GLA_GATE_RANK = 16
GLA_TAU = 16.0
GLA_CHUNK = 64

AB_WIDTHS = (MLA_Q_RANK, MLA_KV_RANK, MLA_ROPE,
             GLA_HEADS * GLA_DK, GLA_HEADS * GLA_DK, GLA_HEADS * GLA_DV, GLA_HEADS * GLA_DV,
             GLA_GATE_RANK, GLA_GATE_RANK)
AB_IN = sum(AB_WIDTHS)
AB_OUT = MLA_HEADS * MLA_V + GLA_HEADS * GLA_DV

SGU_CHUNK = 128
SGU_HIDDEN = 6 * D_MODEL
SGU_HALF = SGU_HIDDEN // 2
SGU_GROUPS = 8
SGU_GROUP_DIM = SGU_HALF // SGU_GROUPS

FFN_HIDDEN = 2816

kernel_name = 'hybrid_mla_gla_sgu_macaron_deepnorm_encoder'


def layer_norm(x, g, b):
    xf = x.astype(jnp.float32)
    mu = jnp.mean(xf, axis=-1, keepdims=True)
    var = jnp.mean(jnp.square(xf - mu), axis=-1, keepdims=True)
    return ((xf - mu) * lax.rsqrt(var + LN_EPS) * g.astype(jnp.float32) + b.astype(jnp.float32)).astype(x.dtype)


def rms_norm(x, g):
    xf = x.astype(jnp.float32)
    ms = jnp.mean(jnp.square(xf), axis=-1, keepdims=True)
    return (xf * lax.rsqrt(ms + RMS_EPS) * g.astype(jnp.float32)).astype(x.dtype)


def split_cols(t, widths):
    out, start = [], 0
    for w in widths:
        out.append(t[..., start:start + w])
        start += w
    return out


def swiglu(h, w_gu, w_down):
    gate, up = jnp.split(h @ w_gu, 2, axis=-1)
    return (jax.nn.silu(gate) * up) @ w_down


def rope_tables(seq, dtype):
    inv_freq = 1.0 / (ROPE_BASE ** (jnp.arange(0, MLA_ROPE, 2, dtype=jnp.float32) / MLA_ROPE))
    ang = jnp.arange(seq, dtype=jnp.float32)[:, None] * inv_freq[None, :]
    return jnp.cos(ang).astype(dtype), jnp.sin(ang).astype(dtype)


def apply_rope(x, cos, sin):
    x1, x2 = jnp.split(x, 2, axis=-1)
    return jnp.concatenate([x1 * cos - x2 * sin, x1 * sin + x2 * cos], axis=-1)


def mla_attention(c_q, c_kv, k_r, q_norm, w_uq, kv_norm, w_ukv):
    B, S, _ = c_q.shape
    H = MLA_HEADS
    q = (rms_norm(c_q, q_norm) @ w_uq).reshape(B, S, H, MLA_NOPE + MLA_ROPE)
    kv = (rms_norm(c_kv, kv_norm) @ w_ukv).reshape(B, S, H, MLA_NOPE + MLA_V)
    q_nope, q_rope = q[..., :MLA_NOPE], q[..., MLA_NOPE:]
    k_nope, v = kv[..., :MLA_NOPE], kv[..., MLA_NOPE:]
    cos, sin = rope_tables(S, q.dtype)
    q_rope = apply_rope(q_rope, cos[:, None, :], sin[:, None, :])
    k_rope = apply_rope(k_r, cos, sin)
    scale = (MLA_NOPE + MLA_ROPE) ** -0.5
    nb = S // Q_BLOCK
    qn_b = jnp.moveaxis(q_nope.reshape(B, nb, Q_BLOCK, H, MLA_NOPE), 1, 0)
    qr_b = jnp.moveaxis(q_rope.reshape(B, nb, Q_BLOCK, H, MLA_ROPE), 1, 0)

    def block(args):
        qn, qr = args
        s = (jnp.einsum('bqhd,bkhd->bhqk', qn, k_nope)
             + jnp.einsum('bqhr,bkr->bhqk', qr, k_rope))
        p = jax.nn.softmax(s.astype(jnp.float32) * scale, axis=-1).astype(v.dtype)
        return jnp.einsum('bhqk,bkhd->bqhd', p, v)

    o = lax.map(block, (qn_b, qr_b))
    return jnp.moveaxis(o, 0, 1).reshape(B, S, H * MLA_V)


def gla_chunked(q, k, v, log_a, strict):
    B, S, H, DK = q.shape
    DV = v.shape[-1]
    C = GLA_CHUNK
    n = S // C
    q, k, log_a = (t.reshape(B, n, C, H, DK) for t in (q, k, log_a))
    v = v.reshape(B, n, C, H, DV)
    b = lax.cumsum(log_a, axis=2)
    b_last = b[:, :, -1:]
    q_in = q * jnp.exp(b)
    k_in = k * jnp.exp(-b)
    k_st = k * jnp.exp(b_last - b)
    mask = jnp.tril(jnp.ones((C, C), dtype=bool), k=-1 if strict else 0)
    att = jnp.where(mask, jnp.einsum('bnthd,bnshd->bnhts', q_in, k_in), 0.0)
    o_intra = jnp.einsum('bnhts,bnshe->bnthe', att, v)
    d_state = jnp.einsum('bnshd,bnshe->bnhde', k_st, v)
    decay = jnp.exp(b_last[:, :, 0])

    def step(state, inp):
        ds, dec = inp
        return dec[..., None] * state + ds, state

    s0 = jnp.zeros((B, H, DK, DV), dtype=q.dtype)
    _, s_before = lax.scan(step, s0, (jnp.moveaxis(d_state, 1, 0), jnp.moveaxis(decay, 1, 0)))
    s_before = jnp.moveaxis(s_before, 0, 1)
    o_inter = jnp.einsum('bnthd,bnhde->bnthe', q_in, s_before)
    return (o_intra + o_inter).reshape(B, S, H, DV)


def gla_bidirectional(q, k, v, r, z_f, z_b, w_gate_f, b_gate_f, w_gate_b, b_gate_b, norm_g):
    B, S, _ = q.shape
    H = GLA_HEADS
    f32 = jnp.float32
    qf = q.astype(f32).reshape(B, S, H, GLA_DK) * (GLA_DK ** -0.5)
    kf = k.astype(f32).reshape(B, S, H, GLA_DK)
    vf = v.astype(f32).reshape(B, S, H, GLA_DV)
    la_f = (jax.nn.log_sigmoid((z_f @ w_gate_f + b_gate_f).astype(f32)) / GLA_TAU).reshape(B, S, H, GLA_DK)
    la_b = (jax.nn.log_sigmoid((z_b @ w_gate_b + b_gate_b).astype(f32)) / GLA_TAU).reshape(B, S, H, GLA_DK)
    flip = lambda t: jnp.flip(t, axis=1)
    o_fwd = gla_chunked(qf, kf, vf, la_f, strict=False)
    o_bwd = flip(gla_chunked(flip(qf), flip(kf), flip(vf), flip(la_b), strict=True))
    o = rms_norm(o_fwd + o_bwd, norm_g)
    return o.reshape(B, S, H * GLA_DV).astype(r.dtype) * jax.nn.silu(r)


def mixer_ab(h, w_in, mla_q_norm, mla_w_uq, mla_kv_norm, mla_w_ukv,
             gla_w_gate_f, gla_b_gate_f, gla_w_gate_b, gla_b_gate_b, gla_norm, w_out):
    c_q, c_kv, k_r, q, k, v, r, z_f, z_b = split_cols(h @ w_in, AB_WIDTHS)
    o_a = mla_attention(c_q, c_kv, k_r, mla_q_norm, mla_w_uq, mla_kv_norm, mla_w_ukv)
    o_b = gla_bidirectional(q, k, v, r, z_f, z_b, gla_w_gate_f, gla_b_gate_f,
                            gla_w_gate_b, gla_b_gate_b, gla_norm)
    return jnp.concatenate([o_a, o_b], axis=-1) @ w_out


def mixer_c(h, w_in, ln_g, ln_b, w_s, b_s, w_out):
    B, S, _ = h.shape
    n = S // SGU_CHUNK
    u, v = jnp.split(jax.nn.gelu(h @ w_in, approximate=False), 2, axis=-1)
    v = layer_norm(v, ln_g, ln_b).reshape(B, n, SGU_CHUNK, SGU_GROUPS, SGU_GROUP_DIM)
    v = jnp.einsum('gts,bnsgc->bntgc', w_s, v) + jnp.transpose(b_s)[None, None, :, :, None]
    return (u * v.reshape(B, S, SGU_HALF)) @ w_out


def post_norm(x, f, g, b):
    return layer_norm(DN_ALPHA * x + f, g, b)


def layer_even(x, ffa_gu, ffa_down, ln1_g, ln1_b, w_in, mla_q_norm, mla_w_uq, mla_kv_norm,
               mla_w_ukv, gla_w_gate_f, gla_b_gate_f, gla_w_gate_b, gla_b_gate_b, gla_norm, w_out,
               ln2_g, ln2_b, ffb_gu, ffb_down, ln3_g, ln3_b):
    x = post_norm(x, 0.5 * swiglu(x, ffa_gu, ffa_down), ln1_g, ln1_b)
    x = post_norm(x, mixer_ab(x, w_in, mla_q_norm, mla_w_uq, mla_kv_norm, mla_w_ukv,
                              gla_w_gate_f, gla_b_gate_f, gla_w_gate_b, gla_b_gate_b,
                              gla_norm, w_out), ln2_g, ln2_b)
    return post_norm(x, 0.5 * swiglu(x, ffb_gu, ffb_down), ln3_g, ln3_b)


def layer_odd(x, ffa_gu, ffa_down, ln1_g, ln1_b, sgu_w_in, sgu_ln_g, sgu_ln_b, sgu_w_s, sgu_b_s,
              sgu_w_out, ln2_g, ln2_b, ffb_gu, ffb_down, ln3_g, ln3_b):
    x = post_norm(x, 0.5 * swiglu(x, ffa_gu, ffa_down), ln1_g, ln1_b)
    x = post_norm(x, mixer_c(x, sgu_w_in, sgu_ln_g, sgu_ln_b, sgu_w_s, sgu_b_s, sgu_w_out), ln2_g, ln2_b)
    return post_norm(x, 0.5 * swiglu(x, ffb_gu, ffb_down), ln3_g, ln3_b)


def _dense(key, fan_in, fan_out, scale=1.0):
    return jax.random.normal(key, (fan_in, fan_out), jnp.float32) * (scale * fan_in ** -0.5)


def _gain(key, shape):
    return 1.0 + 0.02 * jax.random.normal(key, shape, jnp.float32)


def _bias(key, shape, scale=0.02):
    return scale * jax.random.normal(key, shape, jnp.float32)


def setup_inputs(seed: int = 0) -> dict:
    key = jax.random.key(seed)
    ks = iter(jax.random.split(key, 64))
    d = D_MODEL
    inp = {}
    inp['x_prompt'] = jax.random.normal(next(ks), (BATCH, SEQ, d), jnp.float32)
    inp['x_sample'] = jax.random.normal(next(ks), (DEC_BATCH, DEC_SEQ, d), jnp.float32)
    inp['l0_ffa_w_gu'] = _dense(next(ks), d, 2 * FFN_HIDDEN)
    inp['l0_ffa_w_down'] = _dense(next(ks), FFN_HIDDEN, d, DN_BETA)
    inp['l0_ln1_g'] = _gain(next(ks), (d,))
    inp['l0_ln1_b'] = _bias(next(ks), (d,))
    inp['l0_w_in'] = _dense(next(ks), d, AB_IN)
    inp['l0_mla_q_norm'] = _gain(next(ks), (MLA_Q_RANK,))
    inp['l0_mla_w_uq'] = _dense(next(ks), MLA_Q_RANK, MLA_HEADS * (MLA_NOPE + MLA_ROPE))
    inp['l0_mla_kv_norm'] = _gain(next(ks), (MLA_KV_RANK,))
    inp['l0_mla_w_ukv'] = _dense(next(ks), MLA_KV_RANK, MLA_HEADS * (MLA_NOPE + MLA_V))
    inp['l0_gla_w_gate_f'] = _dense(next(ks), GLA_GATE_RANK, GLA_HEADS * GLA_DK)
    inp['l0_gla_b_gate_f'] = _bias(next(ks), (GLA_HEADS * GLA_DK,), 0.1)
    inp['l0_gla_w_gate_b'] = _dense(next(ks), GLA_GATE_RANK, GLA_HEADS * GLA_DK)
    inp['l0_gla_b_gate_b'] = _bias(next(ks), (GLA_HEADS * GLA_DK,), 0.1)
    inp['l0_gla_norm'] = _gain(next(ks), (GLA_DV,))
    inp['l0_w_out'] = _dense(next(ks), AB_OUT, d, DN_BETA)
    inp['l0_ln2_g'] = _gain(next(ks), (d,))
    inp['l0_ln2_b'] = _bias(next(ks), (d,))
    inp['l0_ffb_w_gu'] = _dense(next(ks), d, 2 * FFN_HIDDEN)
    inp['l0_ffb_w_down'] = _dense(next(ks), FFN_HIDDEN, d, DN_BETA)
    inp['l0_ln3_g'] = _gain(next(ks), (d,))
    inp['l0_ln3_b'] = _bias(next(ks), (d,))
    inp['l1_ffa_w_gu'] = _dense(next(ks), d, 2 * FFN_HIDDEN)
    inp['l1_ffa_w_down'] = _dense(next(ks), FFN_HIDDEN, d, DN_BETA)
    inp['l1_ln1_g'] = _gain(next(ks), (d,))
    inp['l1_ln1_b'] = _bias(next(ks), (d,))
    inp['l1_sgu_w_in'] = _dense(next(ks), d, SGU_HIDDEN)
    inp['l1_sgu_ln_g'] = _gain(next(ks), (SGU_HALF,))
    inp['l1_sgu_ln_b'] = _bias(next(ks), (SGU_HALF,))
    inp['l1_sgu_w_s'] = jax.random.normal(next(ks), (SGU_GROUPS, SGU_CHUNK, SGU_CHUNK), jnp.float32) * (SGU_CHUNK ** -0.5)
    inp['l1_sgu_b_s'] = _gain(next(ks), (SGU_GROUPS, SGU_CHUNK))
    inp['l1_sgu_w_out'] = _dense(next(ks), SGU_HALF, d, DN_BETA)
    inp['l1_ln2_g'] = _gain(next(ks), (d,))
    inp['l1_ln2_b'] = _bias(next(ks), (d,))
    inp['l1_ffb_w_gu'] = _dense(next(ks), d, 2 * FFN_HIDDEN)
    inp['l1_ffb_w_down'] = _dense(next(ks), FFN_HIDDEN, d, DN_BETA)
    inp['l1_ln3_g'] = _gain(next(ks), (d,))
    inp['l1_ln3_b'] = _bias(next(ks), (d,))
    return inp


def reference(x_prompt, x_sample,
              l0_ffa_w_gu, l0_ffa_w_down, l0_ln1_g, l0_ln1_b, l0_w_in, l0_mla_q_norm, l0_mla_w_uq,
              l0_mla_kv_norm, l0_mla_w_ukv, l0_gla_w_gate_f, l0_gla_b_gate_f, l0_gla_w_gate_b,
              l0_gla_b_gate_b, l0_gla_norm, l0_w_out, l0_ln2_g, l0_ln2_b, l0_ffb_w_gu, l0_ffb_w_down,
              l0_ln3_g, l0_ln3_b,
              l1_ffa_w_gu, l1_ffa_w_down, l1_ln1_g, l1_ln1_b, l1_sgu_w_in, l1_sgu_ln_g, l1_sgu_ln_b,
              l1_sgu_w_s, l1_sgu_b_s, l1_sgu_w_out, l1_ln2_g, l1_ln2_b, l1_ffb_w_gu, l1_ffb_w_down,
              l1_ln3_g, l1_ln3_b):
    even_params = (l0_ffa_w_gu, l0_ffa_w_down, l0_ln1_g, l0_ln1_b, l0_w_in, l0_mla_q_norm, l0_mla_w_uq,
                   l0_mla_kv_norm, l0_mla_w_ukv, l0_gla_w_gate_f, l0_gla_b_gate_f, l0_gla_w_gate_b,
                   l0_gla_b_gate_b, l0_gla_norm, l0_w_out, l0_ln2_g, l0_ln2_b, l0_ffb_w_gu,
                   l0_ffb_w_down, l0_ln3_g, l0_ln3_b)
    odd_params = (l1_ffa_w_gu, l1_ffa_w_down, l1_ln1_g, l1_ln1_b, l1_sgu_w_in, l1_sgu_ln_g, l1_sgu_ln_b,
                  l1_sgu_w_s, l1_sgu_b_s, l1_sgu_w_out, l1_ln2_g, l1_ln2_b, l1_ffb_w_gu, l1_ffb_w_down,
                  l1_ln3_g, l1_ln3_b)
    layer_params = (even_params, odd_params)

    def trunk(x):
        for i in range(DEPTH):
            if i % 2 == 0:
                x = layer_even(x, *layer_params[i])
            else:
                x = layer_odd(x, *layer_params[i])
        return x

    y_prompt = trunk(x_prompt)
    y_sample = trunk(x_sample)
    return (y_prompt, y_sample)
```

```python
import functools
import math

import jax
import jax.numpy as jnp
from jax import lax
from jax.experimental import pallas as pl
from jax.experimental.pallas import tpu as pltpu

F32 = jnp.float32
BF16 = jnp.bfloat16

D_MODEL = 1024
DEPTH = 2
DN_ALPHA = (2 * DEPTH) ** 0.25
LN_EPS = 1e-5
RMS_EPS = 1e-6

MLA_HEADS = 8
MLA_Q_RANK = 768
MLA_KV_RANK = 256
MLA_NOPE = 64
MLA_ROPE = 32
MLA_V = 64
MLA_HEAD_PAD = 128
ROPE_BASE = 10000.0

GLA_HEADS = 4
GLA_DK = 64
GLA_DV = 128
GLA_GATE_RANK = 16
GLA_TAU = 16.0
GLA_CHUNK = 64
GLA_QK = GLA_HEADS * GLA_DK
GLA_VW = GLA_HEADS * GLA_DV

SGU_CHUNK = 128
SGU_HALF = 3072
SGU_GROUPS = 8
SGU_GROUP_DIM = SGU_HALF // SGU_GROUPS
SGU_COLS = 2 * SGU_GROUP_DIM

FFN_HIDDEN = 2816
FFN_CHUNK = 256

LANE = 128
VMEM_LIMIT_BYTES = 56 * 1024 * 1024

_P_CQ = 0
_P_CKV = _P_CQ + MLA_Q_RANK
_P_GQ = _P_CKV + MLA_KV_RANK
_P_GK = _P_GQ + GLA_QK
_P_GV = _P_GK + GLA_QK
_P_GR = _P_GV + GLA_VW
_P_M1 = _P_GR + GLA_VW
_P_M2 = _P_M1 + LANE
_P_END = _P_M2 + LANE


def _resident(shape):
    zeros = (0,) * len(shape)
    return pl.BlockSpec(shape, lambda *_: zeros, pipeline_mode=pl.Buffered(1))


def _params(n_axes):
    return pltpu.CompilerParams(
        dimension_semantics=("arbitrary",) * n_axes,
        vmem_limit_bytes=VMEM_LIMIT_BYTES,
    )


def _layer_norm(y, g, b):
    mu = jnp.mean(y, axis=-1, keepdims=True)
    yc = y - mu
    var = jnp.mean(yc * yc, axis=-1, keepdims=True)
    return yc * lax.rsqrt(var + LN_EPS) * g + b


def _rms_norm(y, g):
    ms = jnp.mean(y * y, axis=-1, keepdims=True)
    return y * lax.rsqrt(ms + RMS_EPS) * g


def _gelu(x):
    return 0.5 * x * (1.0 + lax.erf(x * math.sqrt(0.5)))


def _silu(x):
    return x * jax.nn.sigmoid(x)


def _log_sigmoid(z):
    return jnp.minimum(z, 0.0) - jnp.log1p(jnp.exp(-jnp.abs(z)))


def _dot(a, b):
    return jnp.dot(a, b, preferred_element_type=F32)


def _dot_nt(a, b):
    return lax.dot_general(a, b, (((1,), (1,)), ((), ())), preferred_element_type=F32)


def _dot_tn(a, b):
    return lax.dot_general(a, b, (((0,), (0,)), ((), ())), preferred_element_type=F32)


def _ffn_ln_kernel(x_ref, wgu_ref, wd_ref, g_ref, b_ref, o_ref, a_ref):
    x = x_ref[...]
    xb = x.astype(BF16)
    n_chunks = wgu_ref.shape[0]
    for c in range(n_chunks):
        h = _dot(xb, wgu_ref[c])
        gate = h[:, :FFN_CHUNK]
        up = h[:, FFN_CHUNK:]
        a_ref[:, c * FFN_CHUNK:(c + 1) * FFN_CHUNK] = (_silu(gate) * up).astype(BF16)
    y = _dot(a_ref[...], wd_ref[...])
    o_ref[...] = _layer_norm(DN_ALPHA * x + 0.5 * y, g_ref[...], b_ref[...])


def _ffn_ln(x2, wgu, wd, g, b, tm):
    n = x2.shape[0]
    row = pl.BlockSpec((tm, D_MODEL), lambda i: (i, 0))
    return pl.pallas_call(
        _ffn_ln_kernel,
        grid=(n // tm,),
        in_specs=[row, _resident(wgu.shape), _resident(wd.shape),
                  _resident(g.shape), _resident(b.shape)],
        out_specs=row,
        out_shape=jax.ShapeDtypeStruct((n, D_MODEL), F32),
        scratch_shapes=[pltpu.VMEM((tm, FFN_HIDDEN), BF16)],
        compiler_params=_params(1),
        name="ffn_ln",
    )(x2, wgu, wd, g, b)


def _proj_ab_kernel(x_ref, win_ref, tab_ref, qn_ref, wqa_ref, wqb_ref, kvn_ref, wka_ref,
                    wvt_ref, wg_ref, bg_ref,
                    q_ref, k_ref, vt_ref, gq_ref, gk_ref, gv_ref, gr_ref, la_ref):
    xb = x_ref[...].astype(BF16)
    p = _dot(xb, win_ref[...])
    gq_ref[...] = p[:, _P_GQ:_P_GK]
    gk_ref[...] = p[:, _P_GK:_P_GV]
    gv_ref[...] = p[:, _P_GV:_P_GR].astype(BF16)
    gr_ref[...] = p[:, _P_GR:_P_M1].astype(BF16)
    m1 = p[:, _P_M1:_P_M2]
    m2 = p[:, _P_M2:_P_END]

    qn = _rms_norm(p[:, _P_CQ:_P_CKV], qn_ref[...]).astype(BF16)
    cq = jnp.concatenate([tab_ref[0]] * MLA_HEADS, axis=1)
    sq = jnp.concatenate([tab_ref[1]] * MLA_HEADS, axis=1)
    q_ref[...] = (_dot(qn, wqa_ref[...]) * cq + _dot(qn, wqb_ref[...]) * sq).astype(BF16)

    kvn = _rms_norm(p[:, _P_CKV:_P_GQ], kvn_ref[...]).astype(BF16)
    k_rope = m1 * tab_ref[2] + m2 * tab_ref[3]
    k_ref[...] = (_dot(kvn, wka_ref[...])
                  + jnp.concatenate([k_rope] * MLA_HEADS, axis=1)).astype(BF16)
    vt_ref[0] = _dot_nt(wvt_ref[...], kvn).astype(BF16)

    m1b = m1.astype(BF16)
    for d in range(2):
        z = _dot(m1b, wg_ref[d]) + bg_ref[d]
        la_ref[d] = _log_sigmoid(z) * (1.0 / GLA_TAU)


def _proj_ab(x2, batch, seq, w, tm):
    n = x2.shape[0]
    nt = seq // tm
    row = lambda width: pl.BlockSpec((tm, width), lambda i: (i, 0))
    in_specs = [
        row(D_MODEL),
        _resident(w["win"].shape),
        pl.BlockSpec((4, tm, LANE), lambda i: (0, i % nt, 0)),
        _resident(w["q_norm"].shape), _resident(w["wqa"].shape), _resident(w["wqb"].shape),
        _resident(w["kv_norm"].shape), _resident(w["wka"].shape), _resident(w["wvt"].shape),
        _resident(w["wg"].shape), _resident(w["bg"].shape),
    ]
    hp = MLA_HEADS * MLA_HEAD_PAD
    hv = MLA_HEADS * MLA_V
    out_specs = [
        row(hp), row(hp),
        pl.BlockSpec((1, hv, tm), lambda i: (i // nt, 0, i % nt)),
        row(GLA_QK), row(GLA_QK), row(GLA_VW), row(GLA_VW),
        pl.BlockSpec((2, tm, GLA_QK), lambda i: (0, i, 0)),
    ]
    out_shape = [
        jax.ShapeDtypeStruct((n, hp), BF16), jax.ShapeDtypeStruct((n, hp), BF16),
        jax.ShapeDtypeStruct((batch, hv, seq), BF16),
        jax.ShapeDtypeStruct((n, GLA_QK), F32), jax.ShapeDtypeStruct((n, GLA_QK), F32),
        jax.ShapeDtypeStruct((n, GLA_VW), BF16), jax.ShapeDtypeStruct((n, GLA_VW), BF16),
        jax.ShapeDtypeStruct((2, n, GLA_QK), F32),
    ]
    return pl.pallas_call(
        _proj_ab_kernel,
        grid=(n // tm,),
        in_specs=in_specs,
        out_specs=out_specs,
        out_shape=out_shape,
        compiler_params=_params(1),
        name="proj_ab",
    )(x2, w["win"], w["rope_tab"][seq], w["q_norm"], w["wqa"], w["wqb"], w["kv_norm"],
      w["wka"], w["wvt"], w["wg"], w["bg"])


def _mla_attn_kernel(q_ref, k_ref, vt_ref, o_ref):
    s_t = _dot_nt(k_ref[0], q_ref[0])
    m = jnp.max(s_t, axis=0, keepdims=True)
    p = jnp.exp2(s_t - m)
    l = jnp.sum(p, axis=0, keepdims=True)
    o_t = _dot(vt_ref[0], p.astype(BF16))
    o_ref[0] = (o_t / l).astype(BF16)


def _mla_attn(q, k, vt, tq):
    batch, seq, _ = q.shape
    return pl.pallas_call(
        _mla_attn_kernel,
        grid=(batch, MLA_HEADS, seq // tq),
        in_specs=[
            pl.BlockSpec((1, tq, MLA_HEAD_PAD), lambda b, h, i: (b, i, h)),
            pl.BlockSpec((1, seq, MLA_HEAD_PAD), lambda b, h, i: (b, 0, h)),
            pl.BlockSpec((1, MLA_V, seq), lambda b, h, i: (b, h, 0)),
        ],
        out_specs=pl.BlockSpec((1, MLA_V, tq), lambda b, h, i: (b, h, i)),
        out_shape=jax.ShapeDtypeStruct((batch, MLA_HEADS * MLA_V, seq), BF16),
        compiler_params=_params(3),
        name="mla_attn",
    )(q, k, vt)


def _split3(x):
    hi = x.astype(BF16)
    r1 = x - hi.astype(F32)
    mid = r1.astype(BF16)
    lo = (r1 - mid.astype(F32)).astype(BF16)
    return hi, mid, lo


def _gla_kernel(q_ref, k_ref, v_ref, r_ref, la_ref, ng_ref, o_ref,
                state_ref, ofwd_ref, oblk_ref, *, n_blocks, tb):
    phase = pl.program_id(1)
    j = pl.program_id(2)
    is_fwd = phase == 0
    n_chunks = tb // GLA_CHUNK
    C = GLA_CHUNK

    @pl.when(j == 0)
    def _():
        state_ref[...] = jnp.zeros_like(state_ref)

    t_idx = lax.broadcasted_iota(jnp.int32, (C, C), 0)
    s_idx = lax.broadcasted_iota(jnp.int32, (C, C), 1)
    sign = jnp.where(is_fwd, 1, -1)
    cum_mat = jnp.where((s_idx - t_idx) * sign <= 0, 1.0, 0.0).astype(BF16)
    t_stack = lax.broadcasted_iota(jnp.int32, (GLA_HEADS * C, C), 0) & (C - 1)
    s_stack = lax.broadcasted_iota(jnp.int32, (GLA_HEADS * C, C), 1)
    att_keep = (s_stack - t_stack) * sign <= jnp.where(is_fwd, 0, -1)
    lane_head = lax.broadcasted_iota(jnp.int32, (C, GLA_QK), 1) // GLA_DK

    for i in range(n_chunks):
        c = jnp.where(is_fwd, i, n_chunks - 1 - i)
        off = pl.multiple_of(c * C, C)
        la = la_ref[0, pl.ds(off, C), :]
        qc = q_ref[pl.ds(off, C), :]
        kc = k_ref[pl.ds(off, C), :]
        vc = v_ref[pl.ds(off, C), :]

        hi, mid, lo = _split3(la)
        b = _dot(cum_mat, hi) + _dot(cum_mat, mid) + _dot(cum_mat, lo)
        b_tot = jnp.sum(la, axis=0, keepdims=True)
        q_in = (qc * jnp.exp(b)).astype(BF16)
        k_in = (kc * jnp.exp(-b)).astype(BF16)
        k_st = (kc * jnp.exp(b_tot - b)).astype(BF16)

        q_heads = [jnp.where(lane_head == h, q_in, jnp.zeros_like(q_in)) for h in range(GLA_HEADS)]
        q_stack = jnp.concatenate(q_heads, axis=0)
        att = _dot_nt(q_stack, k_in)
        att = jnp.where(att_keep, att, 0.0).astype(BF16)

        state = state_ref[...]
        state_b = state.astype(BF16)
        outs = []
        for h in range(GLA_HEADS):
            vh = vc[:, h * GLA_DV:(h + 1) * GLA_DV]
            outs.append(_dot(att[h * C:(h + 1) * C], vh) + _dot(q_heads[h], state_b))
        oblk_ref[pl.ds(off, C), :] = jnp.concatenate(outs, axis=1)

        d_all = _dot_tn(k_st, vc)
        d_state = jnp.concatenate(
            [d_all[h * GLA_DK:(h + 1) * GLA_DK, h * GLA_DV:(h + 1) * GLA_DV]
             for h in range(GLA_HEADS)], axis=0)
        decay = jnp.exp(jnp.transpose(jnp.broadcast_to(b_tot, (GLA_DV, GLA_QK))))
        state_ref[...] = decay * state + d_state

    blk = jnp.where(is_fwd, j, n_blocks - 1 - j)
    row0 = pl.multiple_of(blk * tb, tb)

    @pl.when(is_fwd)
    def _():
        ofwd_ref[pl.ds(row0, tb), :] = oblk_ref[...]

    @pl.when(jnp.logical_not(is_fwd))
    def _():
        o = ofwd_ref[pl.ds(row0, tb), :] + oblk_ref[...]
        ng = ng_ref[...]
        normed = jnp.concatenate(
            [_rms_norm(o[:, h * GLA_DV:(h + 1) * GLA_DV], ng) for h in range(GLA_HEADS)], axis=1)
        o_ref[...] = (normed * _silu(r_ref[...].astype(F32))).astype(BF16)


def _gla(gq, gk, gv, gr, la, norm_g, batch, seq, tb):
    nb = seq // tb

    def blk(ph, j):
        return jnp.where(ph == 0, j, nb - 1 - j)

    def row(width):
        return pl.BlockSpec((tb, width), lambda b, ph, j: (b * nb + blk(ph, j), 0))

    out_spec = pl.BlockSpec((tb, GLA_VW), lambda b, ph, j: (b * nb + nb - 1 - ph * j, 0))
    kern = functools.partial(_gla_kernel, n_blocks=nb, tb=tb)
    return pl.pallas_call(
        kern,
        grid=(batch, 2, nb),
        in_specs=[row(GLA_QK), row(GLA_QK), row(GLA_VW), row(GLA_VW),
                  pl.BlockSpec((1, tb, GLA_QK), lambda b, ph, j: (ph, b * nb + blk(ph, j), 0)),
                  _resident(norm_g.shape)],
        out_specs=out_spec,
        out_shape=jax.ShapeDtypeStruct((batch * seq, GLA_VW), BF16),
        scratch_shapes=[pltpu.VMEM((GLA_QK, GLA_DV), F32),
                        pltpu.VMEM((seq, GLA_VW), F32),
                        pltpu.VMEM((tb, GLA_VW), F32)],
        compiler_params=_params(3),
        name="gla",
    )(gq, gk, gv, gr, la, norm_g)


def _outproj_ln_kernel(x_ref, ot_ref, og_ref, wa_ref, wb_ref, g_ref, b_ref, o_ref):
    y = _dot_tn(ot_ref[0], wa_ref[...]) + _dot(og_ref[...], wb_ref[...])
    o_ref[...] = _layer_norm(DN_ALPHA * x_ref[...] + y, g_ref[...], b_ref[...])


def _outproj_ln(x2, ot, og, wa, wb, g, b, seq, tm):
    n = x2.shape[0]
    nt = seq // tm
    row = lambda width: pl.BlockSpec((tm, width), lambda i: (i, 0))
    return pl.pallas_call(
        _outproj_ln_kernel,
        grid=(n // tm,),
        in_specs=[row(D_MODEL),
                  pl.BlockSpec((1, ot.shape[1], tm), lambda i: (i // nt, 0, i % nt)),
                  row(GLA_VW),
                  _resident(wa.shape), _resident(wb.shape), _resident(g.shape), _resident(b.shape)],
        out_specs=row(D_MODEL),
        out_shape=jax.ShapeDtypeStruct((n, D_MODEL), F32),
        compiler_params=_params(1),
        name="outproj_ln",
    )(x2, ot, og, wa, wb, g, b)


def _sgu_ln_kernel(x_ref, win_ref, lng_ref, lnb_ref, ws_ref, bs_ref, wout_ref, g_ref, b_ref,
                   o_ref, v_ref, vn_ref, p_ref):
    x = x_ref[...]
    xb = x.astype(BF16)
    tm = x.shape[0]
    n_col = SGU_HALF // SGU_COLS
    for jc in range(n_col):
        lo = SGU_HALF + jc * SGU_COLS
        v_ref[:, jc * SGU_COLS:(jc + 1) * SGU_COLS] = _gelu(_dot(xb, win_ref[:, lo:lo + SGU_COLS]))
    vn_ref[...] = _layer_norm(v_ref[...], lng_ref[...], lnb_ref[...]).astype(BF16)

    for jc in range(n_col):
        u = _gelu(_dot(xb, win_ref[:, jc * SGU_COLS:(jc + 1) * SGU_COLS]))
        for gg in range(SGU_COLS // SGU_GROUP_DIM):
            grp = jc * (SGU_COLS // SGU_GROUP_DIM) + gg
            c0 = grp * SGU_GROUP_DIM
            bias = jnp.concatenate([bs_ref[grp]] * (SGU_GROUP_DIM // LANE), axis=1)
            for n in range(tm // SGU_CHUNK):
                r0 = n * SGU_CHUNK
                mixed = _dot(ws_ref[grp], vn_ref[r0:r0 + SGU_CHUNK, c0:c0 + SGU_GROUP_DIM]) + bias
                ug = u[r0:r0 + SGU_CHUNK, gg * SGU_GROUP_DIM:(gg + 1) * SGU_GROUP_DIM]
                p_ref[r0:r0 + SGU_CHUNK, c0:c0 + SGU_GROUP_DIM] = (ug * mixed).astype(BF16)
    y = _dot(p_ref[...], wout_ref[...])
    o_ref[...] = _layer_norm(DN_ALPHA * x + y, g_ref[...], b_ref[...])


def _sgu_ln(x2, win, lng, lnb, ws, bs, wout, g, b, tm):
    n = x2.shape[0]
    row = pl.BlockSpec((tm, D_MODEL), lambda i: (i, 0))
    return pl.pallas_call(
        _sgu_ln_kernel,
        grid=(n // tm,),
        in_specs=[row, _resident(win.shape), _resident(lng.shape), _resident(lnb.shape),
                  _resident(ws.shape), _resident(bs.shape), _resident(wout.shape),
                  _resident(g.shape), _resident(b.shape)],
        out_specs=row,
        out_shape=jax.ShapeDtypeStruct((n, D_MODEL), F32),
        scratch_shapes=[pltpu.VMEM((tm, SGU_HALF), F32),
                        pltpu.VMEM((tm, SGU_HALF), BF16),
                        pltpu.VMEM((tm, SGU_HALF), BF16)],
        compiler_params=_params(1),
        name="sgu_ln",
    )(x2, win, lng, lnb, ws, bs, wout, g, b)


def _row(v):
    return v.reshape(1, -1).astype(F32)


def _pack_ffn(w_gu, w_down):
    nc = FFN_HIDDEN // FFN_CHUNK
    gate = w_gu[:, :FFN_HIDDEN].reshape(D_MODEL, nc, FFN_CHUNK)
    up = w_gu[:, FFN_HIDDEN:].reshape(D_MODEL, nc, FFN_CHUNK)
    wgu = jnp.concatenate([gate, up], axis=2).transpose(1, 0, 2).astype(BF16)
    return wgu, w_down.astype(BF16)


def _rope_tab(seq):
    inv_freq = 1.0 / (ROPE_BASE ** (jnp.arange(0, MLA_ROPE, 2, dtype=F32) / MLA_ROPE))
    ang = jnp.arange(seq, dtype=F32)[:, None] * inv_freq[None, :]
    cos, sin = jnp.cos(ang), jnp.sin(ang)
    pad_lo = jnp.zeros((seq, MLA_NOPE), F32)
    pad_hi = jnp.zeros((seq, MLA_HEAD_PAD - MLA_NOPE - MLA_ROPE), F32)
    k_dir = jnp.concatenate([pad_lo, cos, cos, pad_hi], axis=1)
    k_swp = jnp.concatenate([pad_lo, -sin, sin, pad_hi], axis=1)
    q_scale = (MLA_NOPE + MLA_ROPE) ** -0.5 * math.log2(math.e)
    q_dir = jnp.concatenate([jnp.ones((seq, MLA_NOPE), F32), cos, cos, pad_hi], axis=1) * q_scale
    q_swp = k_swp * q_scale
    return jnp.stack([q_dir, q_swp, k_dir, k_swp])


def _swap_halves(w):
    half = w.shape[-1] // 2
    return jnp.concatenate([w[..., half:], w[..., :half]], axis=-1)


def _pack_even(w_in, q_norm, w_uq, kv_norm, w_ukv, wg_f, bg_f, wg_b, bg_b, seqs):
    o = 0
    parts = {}
    for name, width in (("cq", MLA_Q_RANK), ("ckv", MLA_KV_RANK), ("kr", MLA_ROPE),
                        ("q", GLA_QK), ("k", GLA_QK), ("v", GLA_VW), ("r", GLA_VW),
                        ("zf", GLA_GATE_RANK), ("zb", GLA_GATE_RANK)):
        parts[name] = w_in[:, o:o + width]
        o += width
    zcol = lambda n: jnp.zeros((D_MODEL, n), F32)
    m1 = jnp.concatenate([parts["zf"], parts["zb"], zcol(MLA_NOPE - 2 * GLA_GATE_RANK),
                          parts["kr"], zcol(MLA_HEAD_PAD - MLA_NOPE - MLA_ROPE)], axis=1)
    m2 = jnp.concatenate([zcol(MLA_NOPE), _swap_halves(parts["kr"]),
                          zcol(MLA_HEAD_PAD - MLA_NOPE - MLA_ROPE)], axis=1)
    win = jnp.concatenate([parts["cq"], parts["ckv"], parts["q"] * (GLA_DK ** -0.5), parts["k"],
                           parts["v"], parts["r"], m1, m2], axis=1).astype(BF16)

    qk = MLA_NOPE + MLA_ROPE
    uq = w_uq.reshape(MLA_Q_RANK, MLA_HEADS, qk)
    pad = MLA_HEAD_PAD - qk
    wqa = jnp.pad(uq, ((0, 0), (0, 0), (0, pad))).reshape(MLA_Q_RANK, -1).astype(BF16)
    wqb = jnp.pad(_swap_halves(uq[:, :, MLA_NOPE:]), ((0, 0), (0, 0), (MLA_NOPE, pad)))
    wqb = wqb.reshape(MLA_Q_RANK, -1).astype(BF16)

    ukv = w_ukv.reshape(MLA_KV_RANK, MLA_HEADS, MLA_NOPE + MLA_V)
    wka = jnp.pad(ukv[:, :, :MLA_NOPE], ((0, 0), (0, 0), (0, MLA_HEAD_PAD - MLA_NOPE)))
    wka = wka.reshape(MLA_KV_RANK, -1).astype(BF16)
    wvt = ukv[:, :, MLA_NOPE:].reshape(MLA_KV_RANK, -1).T.astype(BF16)

    gpad = lambda w, at: jnp.pad(w, ((at, LANE - at - GLA_GATE_RANK), (0, 0)))
    wg = jnp.stack([gpad(wg_f, 0), gpad(wg_b, GLA_GATE_RANK)]).astype(BF16)
    bg = jnp.stack([_row(bg_f), _row(bg_b)])
    return dict(win=win, q_norm=_row(q_norm), wqa=wqa, wqb=wqb, kv_norm=_row(kv_norm),
                wka=wka, wvt=wvt, wg=wg, bg=bg,
                rope_tab={s: _rope_tab(s) for s in seqs})


def _tile(n, pref):
    return pref if n % pref == 0 else n


def kernel(x_prompt, x_sample, l0_ffa_w_gu, l0_ffa_w_down, l0_ln1_g, l0_ln1_b, l0_w_in, l0_mla_q_norm, l0_mla_w_uq, l0_mla_kv_norm, l0_mla_w_ukv, l0_gla_w_gate_f, l0_gla_b_gate_f, l0_gla_w_gate_b, l0_gla_b_gate_b, l0_gla_norm, l0_w_out, l0_ln2_g, l0_ln2_b, l0_ffb_w_gu, l0_ffb_w_down, l0_ln3_g, l0_ln3_b, l1_ffa_w_gu, l1_ffa_w_down, l1_ln1_g, l1_ln1_b, l1_sgu_w_in, l1_sgu_ln_g, l1_sgu_ln_b, l1_sgu_w_s, l1_sgu_b_s, l1_sgu_w_out, l1_ln2_g, l1_ln2_b, l1_ffb_w_gu, l1_ffb_w_down, l1_ln3_g, l1_ln3_b):
    seqs = sorted({x_prompt.shape[1], x_sample.shape[1]})
    ffn = [
        _pack_ffn(l0_ffa_w_gu, l0_ffa_w_down) + (_row(l0_ln1_g), _row(l0_ln1_b)),
        _pack_ffn(l0_ffb_w_gu, l0_ffb_w_down) + (_row(l0_ln3_g), _row(l0_ln3_b)),
        _pack_ffn(l1_ffa_w_gu, l1_ffa_w_down) + (_row(l1_ln1_g), _row(l1_ln1_b)),
        _pack_ffn(l1_ffb_w_gu, l1_ffb_w_down) + (_row(l1_ln3_g), _row(l1_ln3_b)),
    ]
    even = _pack_even(l0_w_in, l0_mla_q_norm, l0_mla_w_uq, l0_mla_kv_norm, l0_mla_w_ukv,
                      l0_gla_w_gate_f, l0_gla_b_gate_f, l0_gla_w_gate_b, l0_gla_b_gate_b, seqs)
    hv = MLA_HEADS * MLA_V
    w_out_a = l0_w_out[:hv].astype(BF16)
    w_out_b = l0_w_out[hv:].astype(BF16)
    gla_norm = _row(l0_gla_norm)
    sgu_bs = jnp.broadcast_to(l1_sgu_b_s[:, :, None], (SGU_GROUPS, SGU_CHUNK, LANE)).astype(F32)
    sgu = (l1_sgu_w_in.astype(BF16), _row(l1_sgu_ln_g), _row(l1_sgu_ln_b),
           l1_sgu_w_s.astype(BF16), sgu_bs, l1_sgu_w_out.astype(BF16),
           _row(l1_ln2_g), _row(l1_ln2_b))

    def trunk(x):
        batch, seq, _ = x.shape
        tm = _tile(seq, 512)
        x2 = x.reshape(batch * seq, D_MODEL)
        x2 = _ffn_ln(x2, *ffn[0], tm)
        q, k, vt, gq, gk, gv, gr, la = _proj_ab(x2, batch, seq, even, tm)
        hp = MLA_HEADS * MLA_HEAD_PAD
        ot = _mla_attn(q.reshape(batch, seq, hp), k.reshape(batch, seq, hp), vt, _tile(seq, 512))
        og = _gla(gq, gk, gv, gr, la, gla_norm, batch, seq, _tile(seq, 512))
        x2 = _outproj_ln(x2, ot, og, w_out_a, w_out_b, _row(l0_ln2_g), _row(l0_ln2_b), seq, tm)
        x2 = _ffn_ln(x2, *ffn[1], tm)
        x2 = _ffn_ln(x2, *ffn[2], tm)
        x2 = _sgu_ln(x2, *sgu, _tile(seq, 256))
        x2 = _ffn_ln(x2, *ffn[3], tm)
        return x2.reshape(batch, seq, D_MODEL)

    return (trunk(x_prompt), trunk(x_sample))
```

```python
import functools
import math

import jax
import jax.numpy as jnp
from jax import lax
from jax.experimental import pallas as pl
from jax.experimental.pallas import tpu as pltpu

F32 = jnp.float32
BF16 = jnp.bfloat16

D_MODEL = 1024
DEPTH = 2
DN_ALPHA = (2 * DEPTH) ** 0.25
LN_EPS = 1e-5
RMS_EPS = 1e-6

MLA_HEADS = 8
MLA_Q_RANK = 768
MLA_KV_RANK = 256
MLA_NOPE = 64
MLA_ROPE = 32
MLA_V = 64
MLA_HEAD_PAD = 128
ROPE_BASE = 10000.0
MLA_KEY_CHUNK = 1024
MLA_Q_SUB = 256
MLA_ONES_ROWS = 16

GLA_HEADS = 4
GLA_DK = 64
GLA_DV = 128
GLA_GATE_RANK = 16
GLA_TAU = 16.0
GLA_CHUNK = 64
GLA_GROUP = 256
GLA_QK = GLA_HEADS * GLA_DK
GLA_VW = GLA_HEADS * GLA_DV

SGU_CHUNK = 128
SGU_HALF = 3072
SGU_GROUPS = 8
SGU_GROUP_DIM = SGU_HALF // SGU_GROUPS
SGU_COLS = 2 * SGU_GROUP_DIM
SGU_LN_ROWS = 16

FFN_HIDDEN = 2816
FFN_CHUNK = 256

LANE = 128
LN_ROWS = 32
VMEM_LIMIT_BYTES = 56 * 1024 * 1024

_P_CQ = 0
_P_CKV = _P_CQ + MLA_Q_RANK
_P_GQ = _P_CKV + MLA_KV_RANK
_P_GK = _P_GQ + GLA_QK
_P_GV = _P_GK + GLA_QK
_P_GR = _P_GV + GLA_VW
_P_M1 = _P_GR + GLA_VW
_P_M2 = _P_M1 + LANE
_P_END = _P_M2 + LANE


def _resident(shape):
    zeros = (0,) * len(shape)
    return pl.BlockSpec(shape, lambda *_: zeros, pipeline_mode=pl.Buffered(1))


def _params(n_axes):
    return pltpu.CompilerParams(
        dimension_semantics=("arbitrary",) * n_axes,
        vmem_limit_bytes=VMEM_LIMIT_BYTES,
    )


def _layer_norm(y, g, b):
    mu = jnp.mean(y, axis=-1, keepdims=True)
    yc = y - mu
    var = jnp.mean(yc * yc, axis=-1, keepdims=True)
    return yc * lax.rsqrt(var + LN_EPS) * g + b


def _layer_norm_rows(o_ref, load, g, b):
    for r in range(0, o_ref.shape[0], LN_ROWS):
        rows = slice(r, r + LN_ROWS)
        o_ref[rows, :] = _layer_norm(load(rows), g, b)


def _rms_norm(y, g):
    ms = jnp.mean(y * y, axis=-1, keepdims=True)
    return y * lax.rsqrt(ms + RMS_EPS) * g


def _gelu(x):
    return 0.5 * x * (1.0 + lax.erf(x * math.sqrt(0.5)))


def _silu(x):
    return x * jax.nn.sigmoid(x)


def _log_sigmoid(z):
    return jnp.minimum(z, 0.0) - jnp.log1p(jnp.exp(-jnp.abs(z)))


def _zero_after_store(ref, step):
    row0 = pl.multiple_of(jnp.minimum(step, 0), 8)
    words = pltpu.bitcast(ref[pl.ds(row0, 8), :LANE], jnp.uint32)
    zero = pltpu.bitcast((words >> 16) >> 16, F32)
    return zero[:1, :1]


def _dot(a, b):
    return jnp.dot(a, b, preferred_element_type=F32)


def _dot_nt(a, b):
    return lax.dot_general(a, b, (((1,), (1,)), ((), ())), preferred_element_type=F32)


def _dot_tn(a, b):
    return lax.dot_general(a, b, (((0,), (0,)), ((), ())), preferred_element_type=F32)


def _ffn_ln_kernel(x_ref, wgu_ref, wd_ref, g_ref, b_ref, o_ref, a_ref, z_ref):
    i = pl.program_id(0)

    @pl.when(i == 0)
    def _():
        z_ref[1] = jnp.zeros(z_ref.shape[1:], F32)

    prev = (i + 1) % 2
    tm = o_ref.shape[0]
    n_chunks = FFN_HIDDEN // FFN_CHUNK
    ln_chunks = n_chunks - 3
    ln_rows = tm // ln_chunks if tm % ln_chunks == 0 else tm

    x = x_ref[...]
    xb = x.astype(BF16)
    for c in range(n_chunks):
        lo = c * FFN_CHUNK
        gate = _dot(xb, wgu_ref[:, lo:lo + FFN_CHUNK])
        up = _dot(xb, wgu_ref[:, FFN_HIDDEN + lo:FFN_HIDDEN + lo + FFN_CHUNK])
        if c * ln_rows < tm:
            r0 = c * ln_rows
            _layer_norm_rows(o_ref.at[r0:r0 + ln_rows, :],
                             lambda r: z_ref[prev, r0 + r.start:r0 + r.stop, :],
                             g_ref[...], b_ref[...])
            up = up + _zero_after_store(o_ref, i)
        a_ref[:, lo:lo + FFN_CHUNK] = (_silu(gate) * up).astype(BF16)
    y = _dot(a_ref[...], wd_ref[...])
    z_ref[i % 2] = DN_ALPHA * x + 0.5 * y


def _ffn_ln(x2, wgu, wd, g, b, tm):
    n = x2.shape[0]
    nt = n // tm
    return pl.pallas_call(
        _ffn_ln_kernel,
        grid=(nt + 1,),
        in_specs=[pl.BlockSpec((tm, D_MODEL), lambda i: (jnp.minimum(i, nt - 1), 0)),
                  _resident(wgu.shape), _resident(wd.shape),
                  _resident(g.shape), _resident(b.shape)],
        out_specs=pl.BlockSpec((tm, D_MODEL), lambda i: (jnp.maximum(i - 1, 0), 0)),
        out_shape=jax.ShapeDtypeStruct((n, D_MODEL), F32),
        scratch_shapes=[pltpu.VMEM((tm, FFN_HIDDEN), BF16),
                        pltpu.VMEM((2, tm, D_MODEL), F32)],
        compiler_params=_params(1),
        name="ffn_ln",
    )(x2, wgu, wd, g, b)


def _proj_ab_kernel(x_ref, win_ref, tab_ref, qn_ref, wqa_ref, wqb_ref, kvn_ref, wka_ref,
                    wvt_ref, wg_ref, bg_ref,
                    q_ref, k_ref, vt_ref, gq_ref, gk_ref, gv_ref, gr_ref, la_ref):
    xb = x_ref[...].astype(BF16)
    p = _dot(xb, win_ref[...])
    gq_ref[...] = p[:, _P_GQ:_P_GK]
    gk_ref[...] = p[:, _P_GK:_P_GV]
    gv_ref[...] = p[:, _P_GV:_P_GR].astype(BF16)
    gr_ref[...] = p[:, _P_GR:_P_M1].astype(BF16)
    m1 = p[:, _P_M1:_P_M2]
    m2 = p[:, _P_M2:_P_END]

    qn = _rms_norm(p[:, _P_CQ:_P_CKV], qn_ref[...]).astype(BF16)
    cq = jnp.concatenate([tab_ref[0]] * MLA_HEADS, axis=1)
    sq = jnp.concatenate([tab_ref[1]] * MLA_HEADS, axis=1)
    q_ref[...] = (_dot(qn, wqa_ref[...]) * cq + _dot(qn, wqb_ref[...]) * sq).astype(BF16)

    kvn = _rms_norm(p[:, _P_CKV:_P_GQ], kvn_ref[...]).astype(BF16)
    k_rope = m1 * tab_ref[2] + m2 * tab_ref[3]
    k_ref[...] = (_dot(kvn, wka_ref[...])
                  + jnp.concatenate([k_rope] * MLA_HEADS, axis=1)).astype(BF16)
    vt_ref[0] = _dot_nt(wvt_ref[...], kvn).astype(BF16)

    m1b = m1.astype(BF16)
    for d in range(2):
        z = _dot(m1b, wg_ref[d]) + bg_ref[d]
        la_ref[d] = _log_sigmoid(z) * (1.0 / GLA_TAU)


def _proj_ab(x2, batch, seq, w, tm):
    n = x2.shape[0]
    nt = seq // tm
    row = lambda width: pl.BlockSpec((tm, width), lambda i: (i, 0))
    in_specs = [
        row(D_MODEL),
        _resident(w["win"].shape),
        pl.BlockSpec((4, tm, LANE), lambda i: (0, i % nt, 0)),
        _resident(w["q_norm"].shape), _resident(w["wqa"].shape), _resident(w["wqb"].shape),
        _resident(w["kv_norm"].shape), _resident(w["wka"].shape), _resident(w["wvt"].shape),
        _resident(w["wg"].shape), _resident(w["bg"].shape),
    ]
    hp = MLA_HEADS * MLA_HEAD_PAD
    hv = MLA_HEADS * MLA_V
    out_specs = [
        row(hp), row(hp),
        pl.BlockSpec((1, hv, tm), lambda i: (i // nt, 0, i % nt)),
        row(GLA_QK), row(GLA_QK), row(GLA_VW), row(GLA_VW),
        pl.BlockSpec((2, tm, GLA_QK), lambda i: (0, i, 0)),
    ]
    out_shape = [
        jax.ShapeDtypeStruct((n, hp), BF16), jax.ShapeDtypeStruct((n, hp), BF16),
        jax.ShapeDtypeStruct((batch, hv, seq), BF16),
        jax.ShapeDtypeStruct((n, GLA_QK), F32), jax.ShapeDtypeStruct((n, GLA_QK), F32),
        jax.ShapeDtypeStruct((n, GLA_VW), BF16), jax.ShapeDtypeStruct((n, GLA_VW), BF16),
        jax.ShapeDtypeStruct((2, n, GLA_QK), F32),
    ]
    return pl.pallas_call(
        _proj_ab_kernel,
        grid=(n // tm,),
        in_specs=in_specs,
        out_specs=out_specs,
        out_shape=out_shape,
        compiler_params=_params(1),
        name="proj_ab",
    )(x2, w["win"], w["rope_tab"][seq], w["q_norm"], w["wqa"], w["wqb"], w["kv_norm"],
      w["wka"], w["wvt"], w["wg"], w["bg"])


def _mla_attn_kernel(q_ref, k_ref, vt_ref, o_ref, sa_ref, sb_ref):
    seq = k_ref.shape[1]
    ck = min(MLA_KEY_CHUNK, seq)
    n_chunks = seq // ck
    ones = jnp.ones((MLA_ONES_ROWS, ck), BF16)
    s_refs = (sa_ref, sb_ref)
    dyn0 = jnp.minimum(pl.program_id(0), 0)

    tq = q_ref.shape[1]
    qw = min(MLA_Q_SUB, tq)
    units = [(qt, h) for qt in range(tq // qw) for h in range(MLA_HEADS)]

    def qk_chunk(u, c, m):
        qt, h = units[u]
        lanes = slice(h * MLA_HEAD_PAD, (h + 1) * MLA_HEAD_PAD)
        keys = slice(c * ck, (c + 1) * ck)
        s_t = _dot_nt(k_ref[0, keys, lanes], q_ref[0, qt * qw:(qt + 1) * qw, lanes])
        s_refs[u % 2][pl.ds(pl.multiple_of(c * ck + dyn0, ck), ck), :] = s_t
        mc = jnp.max(s_t, axis=0, keepdims=True)
        return mc if m is None else jnp.maximum(m, mc)

    def pv_chunk(u, c, m, acc):
        _, h = units[u]
        keys = slice(c * ck, (c + 1) * ck)
        s_t = s_refs[u % 2][pl.ds(pl.multiple_of(c * ck + dyn0, ck), ck), :]
        p = jnp.exp2((s_t - m).astype(BF16))
        vt_ext = jnp.concatenate([vt_ref[0, h * MLA_V:(h + 1) * MLA_V, keys], ones], axis=0)
        d = _dot(vt_ext, p)
        return d if acc is None else acc + d

    m = [None] * (len(units) + 1)
    for c in range(n_chunks):
        m[0] = qk_chunk(0, c, m[0])
    for u, (qt, h) in enumerate(units):
        acc = None
        for c in range(n_chunks):
            if u + 1 < len(units):
                m[u + 1] = qk_chunk(u + 1, c, m[u + 1])
            acc = pv_chunk(u, c, m[u], acc)
        o_ref[0, h * MLA_V:(h + 1) * MLA_V, qt * qw:(qt + 1) * qw] = (
            acc[:MLA_V] / acc[MLA_V:MLA_V + 1]).astype(BF16)


def _mla_attn(q, k, vt, tq):
    batch, seq, hp = q.shape
    hv = MLA_HEADS * MLA_V
    return pl.pallas_call(
        _mla_attn_kernel,
        grid=(batch, seq // tq),
        in_specs=[
            pl.BlockSpec((1, tq, hp), lambda b, i: (b, i, 0)),
            pl.BlockSpec((1, seq, hp), lambda b, i: (b, 0, 0)),
            pl.BlockSpec((1, hv, seq), lambda b, i: (b, 0, 0)),
        ],
        out_specs=pl.BlockSpec((1, hv, tq), lambda b, i: (b, 0, i)),
        out_shape=jax.ShapeDtypeStruct((batch, hv, seq), BF16),
        scratch_shapes=[pltpu.VMEM((seq, min(MLA_Q_SUB, tq)), F32)] * 2,
        compiler_params=_params(2),
        name="mla_attn",
    )(q, k, vt)


def _split3(x):
    hi = x.astype(BF16)
    r1 = x - hi.astype(F32)
    mid = r1.astype(BF16)
    lo = (r1 - mid.astype(F32)).astype(BF16)
    return hi, mid, lo


def _gla_kernel(q_ref, k_ref, v_ref, r_ref, la_ref, ng_ref, o_ref,
                state_ref, ofwd_ref, oblk_ref, dstate_ref, decay_ref, states_ref,
                *, n_blocks, tb):
    phase = pl.program_id(1)
    j = pl.program_id(2)
    is_fwd = phase == 0
    n_chunks = tb // GLA_CHUNK
    C = GLA_CHUNK

    @pl.when(j == 0)
    def _():
        state_ref[...] = jnp.zeros_like(state_ref)

    G = min(GLA_GROUP, tb)
    cpg = G // C
    n_groups = tb // G
    H = GLA_HEADS

    sign = jnp.where(is_fwd, 1, -1)
    far = 4 * G

    def rel_matrix(rows):
        t = lax.broadcasted_iota(jnp.int32, (rows, G), 0) & (G - 1)
        s = lax.broadcasted_iota(jnp.int32, (rows, G), 1)
        same_chunk = ((t ^ s) & -C) == 0
        return jnp.where(same_chunk, (s - t) * sign, far)

    cum_mat = jnp.where(rel_matrix(G) <= 0, 1.0, 0.0).astype(BF16)
    att_keep = rel_matrix(H * G) <= jnp.where(is_fwd, 0, -1)
    lane_head_g = lax.broadcasted_iota(jnp.int32, (G, GLA_QK), 1) // GLA_DK
    lane_head_c = lax.broadcasted_iota(jnp.int32, (C, GLA_QK), 1) // GLA_DK

    q_in_chunks, o_intra_chunks = [], []
    for g in range(n_groups):
        rows = slice(g * G, (g + 1) * G)
        la = la_ref[0, rows, :]
        kc = k_ref[rows, :]
        vc = v_ref[rows, :]
        hi, mid, lo = _split3(la)
        b = _dot(cum_mat, hi) + _dot(cum_mat, mid) + _dot(cum_mat, lo)
        tots = [jnp.sum(la[c * C:(c + 1) * C], axis=0, keepdims=True) for c in range(cpg)]
        b_tot = jnp.concatenate([jnp.broadcast_to(t, (C, GLA_QK)) for t in tots], axis=0)
        q_in = (q_ref[rows, :] * jnp.exp(b)).astype(BF16)
        k_in = (kc * jnp.exp(-b)).astype(BF16)
        k_st = (kc * jnp.exp(b_tot - b)).astype(BF16)

        q_stack = jnp.concatenate(
            [jnp.where(lane_head_g == h, q_in, jnp.zeros_like(q_in)) for h in range(H)], axis=0)
        att = jnp.where(att_keep, _dot_nt(q_stack, k_in), 0.0).astype(BF16)
        o_intra = jnp.concatenate(
            [_dot(att[h * G:(h + 1) * G], vc[:, h * GLA_DV:(h + 1) * GLA_DV]) for h in range(H)],
            axis=1)

        for c in range(cpg):
            ch = g * cpg + c
            crow = slice(c * C, (c + 1) * C)
            q_in_chunks.append(q_in[crow])
            o_intra_chunks.append(o_intra[crow])
            d_all = _dot_tn(k_st[crow], vc[crow])
            dstate_ref[ch] = jnp.concatenate(
                [d_all[h * GLA_DK:(h + 1) * GLA_DK, h * GLA_DV:(h + 1) * GLA_DV]
                 for h in range(H)], axis=0)
            decay_ref[ch] = jnp.exp(jnp.transpose(jnp.broadcast_to(tots[c], (GLA_DV, GLA_QK))))

    state = state_ref[...]
    for i in range(n_chunks):
        ch = jnp.where(is_fwd, i, n_chunks - 1 - i)
        states_ref[ch] = state
        state = decay_ref[ch] * state + dstate_ref[ch]
    state_ref[...] = state

    for ch in range(n_chunks):
        q_stack = jnp.concatenate(
            [jnp.where(lane_head_c == h, q_in_chunks[ch], jnp.zeros_like(q_in_chunks[ch]))
             for h in range(H)], axis=0)
        inter = _dot(q_stack, states_ref[ch].astype(BF16))
        oblk_ref[ch * C:(ch + 1) * C, :] = o_intra_chunks[ch] + jnp.concatenate(
            [inter[h * C:(h + 1) * C] for h in range(H)], axis=1)

    blk = jnp.where(is_fwd, j, n_blocks - 1 - j)
    row0 = pl.multiple_of(blk * tb, tb)

    @pl.when(is_fwd)
    def _():
        ofwd_ref[pl.ds(row0, tb), :] = oblk_ref[...]

    @pl.when(jnp.logical_not(is_fwd))
    def _():
        ng = ng_ref[...]
        for r in range(0, tb, C):
            o = ofwd_ref[pl.ds(pl.multiple_of(row0 + r, C), C), :] + oblk_ref[r:r + C, :]
            normed = jnp.concatenate(
                [_rms_norm(o[:, h * GLA_DV:(h + 1) * GLA_DV], ng) for h in range(H)], axis=1)
            o_ref[r:r + C, :] = (normed * _silu(r_ref[r:r + C, :].astype(F32))).astype(BF16)


def _gla(gq, gk, gv, gr, la, norm_g, batch, seq, tb):
    nb = seq // tb

    def blk(ph, j):
        return jnp.where(ph == 0, j, nb - 1 - j)

    def row(width):
        return pl.BlockSpec((tb, width), lambda b, ph, j: (b * nb + blk(ph, j), 0))

    out_spec = pl.BlockSpec((tb, GLA_VW), lambda b, ph, j: (b * nb + nb - 1 - ph * j, 0))
    kern = functools.partial(_gla_kernel, n_blocks=nb, tb=tb)
    return pl.pallas_call(
        kern,
        grid=(batch, 2, nb),
        in_specs=[row(GLA_QK), row(GLA_QK), row(GLA_VW), row(GLA_VW),
                  pl.BlockSpec((1, tb, GLA_QK), lambda b, ph, j: (ph, b * nb + blk(ph, j), 0)),
                  _resident(norm_g.shape)],
        out_specs=out_spec,
        out_shape=jax.ShapeDtypeStruct((batch * seq, GLA_VW), BF16),
        scratch_shapes=[pltpu.VMEM((GLA_QK, GLA_DV), F32),
                        pltpu.VMEM((seq, GLA_VW), F32),
                        pltpu.VMEM((tb, GLA_VW), F32),
                        pltpu.VMEM((tb // GLA_CHUNK, GLA_QK, GLA_DV), F32),
                        pltpu.VMEM((tb // GLA_CHUNK, GLA_QK, GLA_DV), F32),
                        pltpu.VMEM((tb // GLA_CHUNK, GLA_QK, GLA_DV), F32)],
        compiler_params=_params(3),
        name="gla",
    )(gq, gk, gv, gr, la, norm_g)


def _outproj_ln_kernel(x_ref, ot_ref, og_ref, wa_ref, wb_ref, g_ref, b_ref, o_ref):
    y = _dot_tn(ot_ref[0], wa_ref[...]) + _dot(og_ref[...], wb_ref[...])
    _layer_norm_rows(o_ref, lambda r: DN_ALPHA * x_ref[r, :] + y[r, :], g_ref[...], b_ref[...])


def _outproj_ln(x2, ot, og, wa, wb, g, b, seq, tm):
    n = x2.shape[0]
    nt = seq // tm
    row = lambda width: pl.BlockSpec((tm, width), lambda i: (i, 0))
    return pl.pallas_call(
        _outproj_ln_kernel,
        grid=(n // tm,),
        in_specs=[row(D_MODEL),
                  pl.BlockSpec((1, ot.shape[1], tm), lambda i: (i // nt, 0, i % nt)),
                  row(GLA_VW),
                  _resident(wa.shape), _resident(wb.shape), _resident(g.shape), _resident(b.shape)],
        out_specs=row(D_MODEL),
        out_shape=jax.ShapeDtypeStruct((n, D_MODEL), F32),
        compiler_params=_params(1),
        name="outproj_ln",
    )(x2, ot, og, wa, wb, g, b)


def _sgu_ln_kernel(x_ref, win_ref, lng_ref, lnb_ref, ws_ref, bs_ref, wout_ref, g_ref, b_ref,
                   o_ref, v_ref, vn_ref, p_ref, z_ref):
    i = pl.program_id(0)

    @pl.when(i == 0)
    def _():
        z_ref[1] = jnp.zeros(z_ref.shape[1:], F32)

    prev = (i + 1) % 2
    x = x_ref[...]
    xb = x.astype(BF16)
    tm = x.shape[0]
    n_col = SGU_HALF // SGU_COLS
    ln_rows = tm // n_col
    for jc in range(n_col):
        lo = SGU_HALF + jc * SGU_COLS
        v_ref[:, jc * SGU_COLS:(jc + 1) * SGU_COLS] = _gelu(_dot(xb, win_ref[:, lo:lo + SGU_COLS]))
    def u_chunk(jc):
        hu = _dot(xb, win_ref[:, jc * SGU_COLS:(jc + 1) * SGU_COLS])
        r0 = jc * ln_rows
        _layer_norm_rows(o_ref.at[r0:r0 + ln_rows, :],
                         lambda r: z_ref[prev, r0 + r.start:r0 + r.stop, :],
                         g_ref[...], b_ref[...])
        return _gelu(hu + _zero_after_store(o_ref, i))

    u_next = u_chunk(0)
    lng = lng_ref[...]
    lnb = lnb_ref[...]
    for r in range(0, tm, SGU_LN_ROWS):
        vn_ref[r:r + SGU_LN_ROWS, :] = _layer_norm(v_ref[r:r + SGU_LN_ROWS, :], lng, lnb).astype(BF16)

    for jc in range(n_col):
        u = u_next
        if jc + 1 < n_col:
            u_next = u_chunk(jc + 1)
        for gg in range(SGU_COLS // SGU_GROUP_DIM):
            grp = jc * (SGU_COLS // SGU_GROUP_DIM) + gg
            c0 = grp * SGU_GROUP_DIM
            bias = jnp.concatenate([bs_ref[grp]] * (SGU_GROUP_DIM // LANE), axis=1)
            for n in range(tm // SGU_CHUNK):
                t0 = n * SGU_CHUNK
                mixed = _dot(ws_ref[grp], vn_ref[t0:t0 + SGU_CHUNK, c0:c0 + SGU_GROUP_DIM]) + bias
                ug = u[t0:t0 + SGU_CHUNK, gg * SGU_GROUP_DIM:(gg + 1) * SGU_GROUP_DIM]
                p_ref[t0:t0 + SGU_CHUNK, c0:c0 + SGU_GROUP_DIM] = (ug * mixed).astype(BF16)
    y = _dot(p_ref[...], wout_ref[...])
    z_ref[i % 2] = DN_ALPHA * x + y


def _sgu_ln(x2, win, lng, lnb, ws, bs, wout, g, b, tm):
    n = x2.shape[0]
    nt = n // tm
    row = pl.BlockSpec((tm, D_MODEL), lambda i: (jnp.minimum(i, nt - 1), 0))
    return pl.pallas_call(
        _sgu_ln_kernel,
        grid=(nt + 1,),
        in_specs=[row, _resident(win.shape), _resident(lng.shape), _resident(lnb.shape),
                  _resident(ws.shape), _resident(bs.shape), _resident(wout.shape),
                  _resident(g.shape), _resident(b.shape)],
        out_specs=pl.BlockSpec((tm, D_MODEL), lambda i: (jnp.maximum(i - 1, 0), 0)),
        out_shape=jax.ShapeDtypeStruct((n, D_MODEL), F32),
        scratch_shapes=[pltpu.VMEM((tm, SGU_HALF), F32),
                        pltpu.VMEM((tm, SGU_HALF), BF16),
                        pltpu.VMEM((tm, SGU_HALF), BF16),
                        pltpu.VMEM((2, tm, D_MODEL), F32)],
        compiler_params=_params(1),
        name="sgu_ln",
    )(x2, win, lng, lnb, ws, bs, wout, g, b)


def _row(v):
    return v.reshape(1, -1).astype(F32)


def _pack_ffn(w_gu, w_down):
    return w_gu.astype(BF16), w_down.astype(BF16)


def _rope_tab(seq):
    inv_freq = 1.0 / (ROPE_BASE ** (jnp.arange(0, MLA_ROPE, 2, dtype=F32) / MLA_ROPE))
    ang = jnp.arange(seq, dtype=F32)[:, None] * inv_freq[None, :]
    cos, sin = jnp.cos(ang), jnp.sin(ang)
    pad_lo = jnp.zeros((seq, MLA_NOPE), F32)
    pad_hi = jnp.zeros((seq, MLA_HEAD_PAD - MLA_NOPE - MLA_ROPE), F32)
    k_dir = jnp.concatenate([pad_lo, cos, cos, pad_hi], axis=1)
    k_swp = jnp.concatenate([pad_lo, -sin, sin, pad_hi], axis=1)
    q_scale = (MLA_NOPE + MLA_ROPE) ** -0.5 * math.log2(math.e)
    q_dir = jnp.concatenate([jnp.ones((seq, MLA_NOPE), F32), cos, cos, pad_hi], axis=1) * q_scale
    q_swp = k_swp * q_scale
    return jnp.stack([q_dir, q_swp, k_dir, k_swp])


def _swap_halves(w):
    half = w.shape[-1] // 2
    return jnp.concatenate([w[..., half:], w[..., :half]], axis=-1)


def _pack_even(w_in, q_norm, w_uq, kv_norm, w_ukv, wg_f, bg_f, wg_b, bg_b, seqs):
    o = 0
    parts = {}
    for name, width in (("cq", MLA_Q_RANK), ("ckv", MLA_KV_RANK), ("kr", MLA_ROPE),
                        ("q", GLA_QK), ("k", GLA_QK), ("v", GLA_VW), ("r", GLA_VW),
                        ("zf", GLA_GATE_RANK), ("zb", GLA_GATE_RANK)):
        parts[name] = w_in[:, o:o + width]
        o += width
    zcol = lambda n: jnp.zeros((D_MODEL, n), F32)
    m1 = jnp.concatenate([parts["zf"], parts["zb"], zcol(MLA_NOPE - 2 * GLA_GATE_RANK),
                          parts["kr"], zcol(MLA_HEAD_PAD - MLA_NOPE - MLA_ROPE)], axis=1)
    m2 = jnp.concatenate([zcol(MLA_NOPE), _swap_halves(parts["kr"]),
                          zcol(MLA_HEAD_PAD - MLA_NOPE - MLA_ROPE)], axis=1)
    win = jnp.concatenate([parts["cq"], parts["ckv"], parts["q"] * (GLA_DK ** -0.5), parts["k"],
                           parts["v"], parts["r"], m1, m2], axis=1).astype(BF16)

    qk = MLA_NOPE + MLA_ROPE
    uq = w_uq.reshape(MLA_Q_RANK, MLA_HEADS, qk)
    pad = MLA_HEAD_PAD - qk
    wqa = jnp.pad(uq, ((0, 0), (0, 0), (0, pad))).reshape(MLA_Q_RANK, -1).astype(BF16)
    wqb = jnp.pad(_swap_halves(uq[:, :, MLA_NOPE:]), ((0, 0), (0, 0), (MLA_NOPE, pad)))
    wqb = wqb.reshape(MLA_Q_RANK, -1).astype(BF16)

    ukv = w_ukv.reshape(MLA_KV_RANK, MLA_HEADS, MLA_NOPE + MLA_V)
    wka = jnp.pad(ukv[:, :, :MLA_NOPE], ((0, 0), (0, 0), (0, MLA_HEAD_PAD - MLA_NOPE)))
    wka = wka.reshape(MLA_KV_RANK, -1).astype(BF16)
    wvt = ukv[:, :, MLA_NOPE:].reshape(MLA_KV_RANK, -1).T.astype(BF16)

    gpad = lambda w, at: jnp.pad(w, ((at, LANE - at - GLA_GATE_RANK), (0, 0)))
    wg = jnp.stack([gpad(wg_f, 0), gpad(wg_b, GLA_GATE_RANK)]).astype(BF16)
    bg = jnp.stack([_row(bg_f), _row(bg_b)])
    return dict(win=win, q_norm=_row(q_norm), wqa=wqa, wqb=wqb, kv_norm=_row(kv_norm),
                wka=wka, wvt=wvt, wg=wg, bg=bg,
                rope_tab={s: _rope_tab(s) for s in seqs})


def _tile(n, pref):
    return pref if n % pref == 0 else n


def kernel(x_prompt, x_sample, l0_ffa_w_gu, l0_ffa_w_down, l0_ln1_g, l0_ln1_b, l0_w_in, l0_mla_q_norm, l0_mla_w_uq, l0_mla_kv_norm, l0_mla_w_ukv, l0_gla_w_gate_f, l0_gla_b_gate_f, l0_gla_w_gate_b, l0_gla_b_gate_b, l0_gla_norm, l0_w_out, l0_ln2_g, l0_ln2_b, l0_ffb_w_gu, l0_ffb_w_down, l0_ln3_g, l0_ln3_b, l1_ffa_w_gu, l1_ffa_w_down, l1_ln1_g, l1_ln1_b, l1_sgu_w_in, l1_sgu_ln_g, l1_sgu_ln_b, l1_sgu_w_s, l1_sgu_b_s, l1_sgu_w_out, l1_ln2_g, l1_ln2_b, l1_ffb_w_gu, l1_ffb_w_down, l1_ln3_g, l1_ln3_b):
    seqs = sorted({x_prompt.shape[1], x_sample.shape[1]})
    ffn = [
        _pack_ffn(l0_ffa_w_gu, l0_ffa_w_down) + (_row(l0_ln1_g), _row(l0_ln1_b)),
        _pack_ffn(l0_ffb_w_gu, l0_ffb_w_down) + (_row(l0_ln3_g), _row(l0_ln3_b)),
        _pack_ffn(l1_ffa_w_gu, l1_ffa_w_down) + (_row(l1_ln1_g), _row(l1_ln1_b)),
        _pack_ffn(l1_ffb_w_gu, l1_ffb_w_down) + (_row(l1_ln3_g), _row(l1_ln3_b)),
    ]
    even = _pack_even(l0_w_in, l0_mla_q_norm, l0_mla_w_uq, l0_mla_kv_norm, l0_mla_w_ukv,
                      l0_gla_w_gate_f, l0_gla_b_gate_f, l0_gla_w_gate_b, l0_gla_b_gate_b, seqs)
    hv = MLA_HEADS * MLA_V
    w_out_a = l0_w_out[:hv].astype(BF16)
    w_out_b = l0_w_out[hv:].astype(BF16)
    gla_norm = _row(l0_gla_norm)
    sgu_bs = jnp.broadcast_to(l1_sgu_b_s[:, :, None], (SGU_GROUPS, SGU_CHUNK, LANE)).astype(F32)
    sgu = (l1_sgu_w_in.astype(BF16), _row(l1_sgu_ln_g), _row(l1_sgu_ln_b),
           l1_sgu_w_s.astype(BF16), sgu_bs, l1_sgu_w_out.astype(BF16),
           _row(l1_ln2_g), _row(l1_ln2_b))

    def trunk(x):
        batch, seq, _ = x.shape
        tm = _tile(seq, 512)
        x2 = x.reshape(batch * seq, D_MODEL)
        x2 = _ffn_ln(x2, *ffn[0], tm)
        q, k, vt, gq, gk, gv, gr, la = _proj_ab(x2, batch, seq, even, tm)
        hp = MLA_HEADS * MLA_HEAD_PAD
        ot = _mla_attn(q.reshape(batch, seq, hp), k.reshape(batch, seq, hp), vt, _tile(seq, 512))
        og = _gla(gq, gk, gv, gr, la, gla_norm, batch, seq, _tile(seq, 512))
        x2 = _outproj_ln(x2, ot, og, w_out_a, w_out_b, _row(l0_ln2_g), _row(l0_ln2_b), seq, tm)
        x2 = _ffn_ln(x2, *ffn[1], tm)
        x2 = _ffn_ln(x2, *ffn[2], tm)
        x2 = _sgu_ln(x2, *sgu, _tile(seq, 256))
        x2 = _ffn_ln(x2, *ffn[3], tm)
        return x2.reshape(batch, seq, D_MODEL)

    return (trunk(x_prompt), trunk(x_sample))
```

```python
import functools
import math

import jax
import jax.numpy as jnp
from jax import lax
from jax.experimental import pallas as pl
from jax.experimental.pallas import tpu as pltpu

F32 = jnp.float32
BF16 = jnp.bfloat16

D_MODEL = 1024
DEPTH = 2
DN_ALPHA = (2 * DEPTH) ** 0.25
LN_EPS = 1e-5
RMS_EPS = 1e-6

MLA_HEADS = 8
MLA_Q_RANK = 768
MLA_KV_RANK = 256
MLA_NOPE = 64
MLA_ROPE = 32
MLA_V = 64
MLA_HEAD_PAD = 128
ROPE_BASE = 10000.0
MLA_KEY_CHUNK = 1024
MLA_Q_SUB = 256
MLA_ONES_ROWS = 16

GLA_HEADS = 4
GLA_DK = 64
GLA_DV = 128
GLA_GATE_RANK = 16
GLA_TAU = 16.0
GLA_CHUNK = 64
GLA_GROUP = 256
GLA_QK = GLA_HEADS * GLA_DK
GLA_VW = GLA_HEADS * GLA_DV

SGU_CHUNK = 128
SGU_HALF = 3072
SGU_GROUPS = 8
SGU_GROUP_DIM = SGU_HALF // SGU_GROUPS
SGU_COLS = 2 * SGU_GROUP_DIM
SGU_LN_ROWS = 16

FFN_HIDDEN = 2816
FFN_CHUNK = 256

LANE = 128
LN_ROWS = 32
VMEM_LIMIT_BYTES = 56 * 1024 * 1024

_P_CQ = 0
_P_CKV = _P_CQ + MLA_Q_RANK
_P_GQ = _P_CKV + MLA_KV_RANK
_P_GK = _P_GQ + GLA_QK
_P_GV = _P_GK + GLA_QK
_P_GR = _P_GV + GLA_VW
_P_M1 = _P_GR + GLA_VW
_P_END = _P_M1 + LANE


def _resident(shape):
    zeros = (0,) * len(shape)
    return pl.BlockSpec(shape, lambda *_: zeros, pipeline_mode=pl.Buffered(1))


def _params(n_axes):
    return pltpu.CompilerParams(
        dimension_semantics=("arbitrary",) * n_axes,
        vmem_limit_bytes=VMEM_LIMIT_BYTES,
    )


def _layer_norm(y, g, b):
    mu = jnp.mean(y, axis=-1, keepdims=True)
    yc = y - mu
    var = jnp.mean(yc * yc, axis=-1, keepdims=True)
    return yc * lax.rsqrt(var + LN_EPS) * g + b


def _layer_norm_rows(o_ref, load, g, b):
    for r in range(0, o_ref.shape[0], LN_ROWS):
        rows = slice(r, r + LN_ROWS)
        o_ref[rows, :] = _layer_norm(load(rows), g, b)


def _rms_norm(y, g):
    ms = jnp.mean(y * y, axis=-1, keepdims=True)
    return y * lax.rsqrt(ms + RMS_EPS) * g


def _gelu(x):
    return 0.5 * x * (1.0 + lax.erf(x * math.sqrt(0.5)))


def _silu(x):
    return x * jax.nn.sigmoid(x)


def _log_sigmoid(z):
    return jnp.minimum(z, 0.0) - jnp.log1p(jnp.exp(-jnp.abs(z)))


def _zero_after_store(ref, step):
    row0 = pl.multiple_of(jnp.minimum(step, 0), 8)
    words = pltpu.bitcast(ref[pl.ds(row0, 8), :LANE], jnp.uint32)
    zero = pltpu.bitcast((words >> 16) >> 16, F32)
    return zero[:1, :1]


def _dot(a, b):
    return jnp.dot(a, b, preferred_element_type=F32)


def _dot_nt(a, b):
    return lax.dot_general(a, b, (((1,), (1,)), ((), ())), preferred_element_type=F32)


def _dot_tn(a, b):
    return lax.dot_general(a, b, (((0,), (0,)), ((), ())), preferred_element_type=F32)


def _ffn_ln_kernel(x_ref, wgu_ref, wd_ref, g_ref, b_ref, o_ref, a_ref, z_ref):
    i = pl.program_id(0)

    @pl.when(i == 0)
    def _():
        z_ref[1] = jnp.zeros(z_ref.shape[1:], F32)

    prev = (i + 1) % 2
    tm = o_ref.shape[0]
    n_chunks = FFN_HIDDEN // FFN_CHUNK
    ln_chunks = n_chunks - 3
    ln_rows = tm // ln_chunks if tm % ln_chunks == 0 else tm

    x = x_ref[...]
    xb = x.astype(BF16)
    for c in range(n_chunks):
        lo = c * FFN_CHUNK
        gate = _dot(xb, wgu_ref[:, lo:lo + FFN_CHUNK])
        up = _dot(xb, wgu_ref[:, FFN_HIDDEN + lo:FFN_HIDDEN + lo + FFN_CHUNK])
        if c * ln_rows < tm:
            r0 = c * ln_rows
            _layer_norm_rows(o_ref.at[r0:r0 + ln_rows, :],
                             lambda r: z_ref[prev, r0 + r.start:r0 + r.stop, :],
                             g_ref[...], b_ref[...])
            up = up + _zero_after_store(o_ref, i)
        a_ref[:, lo:lo + FFN_CHUNK] = (_silu(gate) * up).astype(BF16)
    y = _dot(a_ref[...], wd_ref[...])
    z_ref[i % 2] = DN_ALPHA * x + 0.5 * y


def _ffn_ln(x2, wgu, wd, g, b, tm):
    n = x2.shape[0]
    nt = n // tm
    return pl.pallas_call(
        _ffn_ln_kernel,
        grid=(nt + 1,),
        in_specs=[pl.BlockSpec((tm, D_MODEL), lambda i: (jnp.minimum(i, nt - 1), 0)),
                  _resident(wgu.shape), _resident(wd.shape),
                  _resident(g.shape), _resident(b.shape)],
        out_specs=pl.BlockSpec((tm, D_MODEL), lambda i: (jnp.maximum(i - 1, 0), 0)),
        out_shape=jax.ShapeDtypeStruct((n, D_MODEL), F32),
        scratch_shapes=[pltpu.VMEM((tm, FFN_HIDDEN), BF16),
                        pltpu.VMEM((2, tm, D_MODEL), F32)],
        compiler_params=_params(1),
        name="ffn_ln",
    )(x2, wgu, wd, g, b)


def _swap_rope_halves(x):
    n = x.shape[1]
    half = MLA_ROPE // 2
    lane = lax.broadcasted_iota(jnp.int32, x.shape, 1) & (MLA_HEAD_PAD - 1)
    from_above = pltpu.roll(x, n - half, axis=1)
    from_below = pltpu.roll(x, half, axis=1)
    return jnp.where(lane < MLA_NOPE + half, from_above, from_below)


def _proj_ab_kernel(x_ref, win_ref, tab_ref, qn_ref, wqa_ref, kvn_ref, wka_ref,
                    wvt_ref, wg_ref, bg_ref,
                    q_ref, k_ref, vt_ref, gq_ref, gk_ref, gv_ref, gr_ref, la_ref):
    xb = x_ref[...].astype(BF16)
    p = _dot(xb, win_ref[...])
    gq_ref[...] = p[:, _P_GQ:_P_GK]
    gk_ref[...] = p[:, _P_GK:_P_GV]
    gv_ref[...] = p[:, _P_GV:_P_GR].astype(BF16)
    gr_ref[...] = p[:, _P_GR:_P_M1].astype(BF16)
    m1 = p[:, _P_M1:_P_END]

    qn = _rms_norm(p[:, _P_CQ:_P_CKV], qn_ref[...]).astype(BF16)
    cq = jnp.concatenate([tab_ref[0]] * MLA_HEADS, axis=1)
    sq = jnp.concatenate([tab_ref[1]] * MLA_HEADS, axis=1)
    qa = _dot(qn, wqa_ref[...])
    q_ref[...] = (qa * cq + _swap_rope_halves(qa) * sq).astype(BF16)

    kvn = _rms_norm(p[:, _P_CKV:_P_GQ], kvn_ref[...]).astype(BF16)
    k_rope = m1 * tab_ref[2] + _swap_rope_halves(m1) * tab_ref[3]
    k_ref[...] = (_dot(kvn, wka_ref[...])
                  + jnp.concatenate([k_rope] * MLA_HEADS, axis=1)).astype(BF16)
    vt_ref[0] = _dot_nt(wvt_ref[...], kvn).astype(BF16)

    m1b = m1.astype(BF16)
    for d in range(2):
        z = _dot(m1b, wg_ref[d]) + bg_ref[d]
        la_ref[d] = _log_sigmoid(z) * (1.0 / GLA_TAU)


def _proj_ab(x2, batch, seq, w, tm):
    n = x2.shape[0]
    nt = seq // tm
    row = lambda width: pl.BlockSpec((tm, width), lambda i: (i, 0))
    in_specs = [
        row(D_MODEL),
        _resident(w["win"].shape),
        pl.BlockSpec((4, tm, LANE), lambda i: (0, i % nt, 0)),
        _resident(w["q_norm"].shape), _resident(w["wqa"].shape),
        _resident(w["kv_norm"].shape), _resident(w["wka"].shape), _resident(w["wvt"].shape),
        _resident(w["wg"].shape), _resident(w["bg"].shape),
    ]
    hp = MLA_HEADS * MLA_HEAD_PAD
    hv = MLA_HEADS * MLA_V
    out_specs = [
        row(hp), row(hp),
        pl.BlockSpec((1, hv, tm), lambda i: (i // nt, 0, i % nt)),
        row(GLA_QK), row(GLA_QK), row(GLA_VW), row(GLA_VW),
        pl.BlockSpec((2, tm, GLA_QK), lambda i: (0, i, 0)),
    ]
    out_shape = [
        jax.ShapeDtypeStruct((n, hp), BF16), jax.ShapeDtypeStruct((n, hp), BF16),
        jax.ShapeDtypeStruct((batch, hv, seq), BF16),
        jax.ShapeDtypeStruct((n, GLA_QK), F32), jax.ShapeDtypeStruct((n, GLA_QK), F32),
        jax.ShapeDtypeStruct((n, GLA_VW), BF16), jax.ShapeDtypeStruct((n, GLA_VW), BF16),
        jax.ShapeDtypeStruct((2, n, GLA_QK), F32),
    ]
    return pl.pallas_call(
        _proj_ab_kernel,
        grid=(n // tm,),
        in_specs=in_specs,
        out_specs=out_specs,
        out_shape=out_shape,
        compiler_params=_params(1),
        name="proj_ab",
    )(x2, w["win"], w["rope_tab"][seq], w["q_norm"], w["wqa"], w["kv_norm"],
      w["wka"], w["wvt"], w["wg"], w["bg"])


def _mla_attn_kernel(q_ref, k_ref, vt_ref, o_ref, sa_ref, sb_ref):
    seq = k_ref.shape[1]
    ck = min(MLA_KEY_CHUNK, seq)
    n_chunks = seq // ck
    ones = jnp.ones((MLA_ONES_ROWS, ck), BF16)
    s_refs = (sa_ref, sb_ref)
    dyn0 = jnp.minimum(pl.program_id(0), 0)

    tq = q_ref.shape[1]
    qw = min(MLA_Q_SUB, tq)
    units = [(qt, h) for qt in range(tq // qw) for h in range(MLA_HEADS)]

    def qk_chunk(u, c, m):
        qt, h = units[u]
        lanes = slice(h * MLA_HEAD_PAD, (h + 1) * MLA_HEAD_PAD)
        keys = slice(c * ck, (c + 1) * ck)
        s_t = _dot_nt(k_ref[0, keys, lanes], q_ref[0, qt * qw:(qt + 1) * qw, lanes])
        s_refs[u % 2][pl.ds(pl.multiple_of(c * ck + dyn0, ck), ck), :] = s_t
        mc = jnp.max(s_t, axis=0, keepdims=True)
        return mc if m is None else jnp.maximum(m, mc)

    def pv_chunk(u, c, m, acc):
        _, h = units[u]
        keys = slice(c * ck, (c + 1) * ck)
        s_t = s_refs[u % 2][pl.ds(pl.multiple_of(c * ck + dyn0, ck), ck), :]
        p = jnp.exp2((s_t - m).astype(BF16))
        vt_ext = jnp.concatenate([vt_ref[0, h * MLA_V:(h + 1) * MLA_V, keys], ones], axis=0)
        d = _dot(vt_ext, p)
        return d if acc is None else acc + d

    m = [None] * (len(units) + 1)
    for c in range(n_chunks):
        m[0] = qk_chunk(0, c, m[0])
    for u, (qt, h) in enumerate(units):
        acc = None
        for c in range(n_chunks):
            if u + 1 < len(units):
                m[u + 1] = qk_chunk(u + 1, c, m[u + 1])
            acc = pv_chunk(u, c, m[u], acc)
        o_ref[0, h * MLA_V:(h + 1) * MLA_V, qt * qw:(qt + 1) * qw] = (
            acc[:MLA_V] / acc[MLA_V:MLA_V + 1]).astype(BF16)


def _mla_attn(q, k, vt, tq):
    batch, seq, hp = q.shape
    hv = MLA_HEADS * MLA_V
    return pl.pallas_call(
        _mla_attn_kernel,
        grid=(batch, seq // tq),
        in_specs=[
            pl.BlockSpec((1, tq, hp), lambda b, i: (b, i, 0)),
            pl.BlockSpec((1, seq, hp), lambda b, i: (b, 0, 0)),
            pl.BlockSpec((1, hv, seq), lambda b, i: (b, 0, 0)),
        ],
        out_specs=pl.BlockSpec((1, hv, tq), lambda b, i: (b, 0, i)),
        out_shape=jax.ShapeDtypeStruct((batch, hv, seq), BF16),
        scratch_shapes=[pltpu.VMEM((seq, min(MLA_Q_SUB, tq)), F32)] * 2,
        compiler_params=_params(2),
        name="mla_attn",
    )(q, k, vt)


def _split3(x):
    hi = x.astype(BF16)
    r1 = x - hi.astype(F32)
    mid = r1.astype(BF16)
    lo = (r1 - mid.astype(F32)).astype(BF16)
    return hi, mid, lo


def _gla_kernel(q_ref, k_ref, v_ref, r_ref, la_ref, ng_ref, o_ref,
                state_ref, ofwd_ref, oblk_ref, dstate_ref, decay_ref, states_ref,
                *, n_blocks, tb):
    phase = pl.program_id(1)
    j = pl.program_id(2)
    is_fwd = phase == 0
    n_chunks = tb // GLA_CHUNK
    C = GLA_CHUNK

    @pl.when(j == 0)
    def _():
        state_ref[...] = jnp.zeros_like(state_ref)

    G = min(GLA_GROUP, tb)
    cpg = G // C
    n_groups = tb // G
    H = GLA_HEADS

    sign = jnp.where(is_fwd, 1, -1)
    far = 4 * G

    def rel_matrix(rows):
        t = lax.broadcasted_iota(jnp.int32, (rows, G), 0) & (G - 1)
        s = lax.broadcasted_iota(jnp.int32, (rows, G), 1)
        same_chunk = ((t ^ s) & -C) == 0
        return jnp.where(same_chunk, (s - t) * sign, far)

    cum_mat = jnp.where(rel_matrix(G) <= 0, 1.0, 0.0).astype(BF16)
    att_keep = rel_matrix(H * G) <= jnp.where(is_fwd, 0, -1)
    lane_head_g = lax.broadcasted_iota(jnp.int32, (G, GLA_QK), 1) // GLA_DK
    lane_head_c = lax.broadcasted_iota(jnp.int32, (C, GLA_QK), 1) // GLA_DK

    groups = range(n_groups)
    rows = [slice(g * G, (g + 1) * G) for g in groups]
    las = [la_ref[0, rows[g], :] for g in groups]
    splits = [_split3(la) for la in las]
    bs = [_dot(cum_mat, hi) + _dot(cum_mat, mid) + _dot(cum_mat, lo) for hi, mid, lo in splits]
    tots = [[jnp.sum(la[c * C:(c + 1) * C], axis=0, keepdims=True) for c in range(cpg)]
            for la in las]
    b_tots = [jnp.concatenate([jnp.broadcast_to(t, (C, GLA_QK)) for t in tots[g]], axis=0)
              for g in groups]
    q_ins = [(q_ref[rows[g], :] * jnp.exp(bs[g])).astype(BF16) for g in groups]
    k_ins = [(k_ref[rows[g], :] * jnp.exp(-bs[g])).astype(BF16) for g in groups]
    k_sts = [(k_ref[rows[g], :] * jnp.exp(b_tots[g] - bs[g])).astype(BF16) for g in groups]
    q_stacks = [jnp.concatenate(
        [jnp.where(lane_head_g == h, q_ins[g], jnp.zeros_like(q_ins[g])) for h in range(H)],
        axis=0) for g in groups]
    atts = [jnp.where(att_keep, _dot_nt(q_stacks[g], k_ins[g]), 0.0).astype(BF16)
            for g in groups]
    for g in groups:
        for c in range(cpg):
            ch = g * cpg + c
            crow = slice(g * G + c * C, g * G + (c + 1) * C)
            d_all = _dot_tn(k_sts[g][c * C:(c + 1) * C], v_ref[crow, :])
            dstate_ref[ch] = jnp.concatenate(
                [d_all[h * GLA_DK:(h + 1) * GLA_DK, h * GLA_DV:(h + 1) * GLA_DV]
                 for h in range(H)], axis=0)
            decay_ref[ch] = jnp.exp(jnp.transpose(jnp.broadcast_to(tots[g][c], (GLA_DV, GLA_QK))))
    o_intras = [jnp.concatenate(
        [_dot(atts[g][h * G:(h + 1) * G], v_ref[rows[g], h * GLA_DV:(h + 1) * GLA_DV])
         for h in range(H)], axis=1) for g in groups]
    q_in_chunks = [q_ins[g][c * C:(c + 1) * C] for g in groups for c in range(cpg)]
    o_intra_chunks = [o_intras[g][c * C:(c + 1) * C] for g in groups for c in range(cpg)]

    state = state_ref[...]
    for i in range(n_chunks):
        ch = jnp.where(is_fwd, i, n_chunks - 1 - i)
        states_ref[ch] = state
        state = decay_ref[ch] * state + dstate_ref[ch]
    state_ref[...] = state

    for ch in range(n_chunks):
        q_stack = jnp.concatenate(
            [jnp.where(lane_head_c == h, q_in_chunks[ch], jnp.zeros_like(q_in_chunks[ch]))
             for h in range(H)], axis=0)
        inter = _dot(q_stack, states_ref[ch].astype(BF16))
        oblk_ref[ch * C:(ch + 1) * C, :] = o_intra_chunks[ch] + jnp.concatenate(
            [inter[h * C:(h + 1) * C] for h in range(H)], axis=1)

    blk = jnp.where(is_fwd, j, n_blocks - 1 - j)
    row0 = pl.multiple_of(blk * tb, tb)

    @pl.when(is_fwd)
    def _():
        ofwd_ref[pl.ds(row0, tb), :] = oblk_ref[...]

    @pl.when(jnp.logical_not(is_fwd))
    def _():
        ng = ng_ref[...]
        for r in range(0, tb, C):
            o = ofwd_ref[pl.ds(pl.multiple_of(row0 + r, C), C), :] + oblk_ref[r:r + C, :]
            normed = jnp.concatenate(
                [_rms_norm(o[:, h * GLA_DV:(h + 1) * GLA_DV], ng) for h in range(H)], axis=1)
            o_ref[r:r + C, :] = (normed * _silu(r_ref[r:r + C, :].astype(F32))).astype(BF16)


def _gla(gq, gk, gv, gr, la, norm_g, batch, seq, tb):
    nb = seq // tb

    def blk(ph, j):
        return jnp.where(ph == 0, j, nb - 1 - j)

    def row(width):
        return pl.BlockSpec((tb, width), lambda b, ph, j: (b * nb + blk(ph, j), 0))

    out_spec = pl.BlockSpec((tb, GLA_VW), lambda b, ph, j: (b * nb + nb - 1 - ph * j, 0))
    kern = functools.partial(_gla_kernel, n_blocks=nb, tb=tb)
    return pl.pallas_call(
        kern,
        grid=(batch, 2, nb),
        in_specs=[row(GLA_QK), row(GLA_QK), row(GLA_VW), row(GLA_VW),
                  pl.BlockSpec((1, tb, GLA_QK), lambda b, ph, j: (ph, b * nb + blk(ph, j), 0)),
                  _resident(norm_g.shape)],
        out_specs=out_spec,
        out_shape=jax.ShapeDtypeStruct((batch * seq, GLA_VW), BF16),
        scratch_shapes=[pltpu.VMEM((GLA_QK, GLA_DV), F32),
                        pltpu.VMEM((seq, GLA_VW), F32),
                        pltpu.VMEM((tb, GLA_VW), F32),
                        pltpu.VMEM((tb // GLA_CHUNK, GLA_QK, GLA_DV), F32),
                        pltpu.VMEM((tb // GLA_CHUNK, GLA_QK, GLA_DV), F32),
                        pltpu.VMEM((tb // GLA_CHUNK, GLA_QK, GLA_DV), F32)],
        compiler_params=_params(3),
        name="gla",
    )(gq, gk, gv, gr, la, norm_g)


def _outproj_ln_kernel(x_ref, ot_ref, og_ref, wa_ref, wb_ref, g_ref, b_ref, o_ref):
    y = _dot_tn(ot_ref[0], wa_ref[...]) + _dot(og_ref[...], wb_ref[...])
    _layer_norm_rows(o_ref, lambda r: DN_ALPHA * x_ref[r, :] + y[r, :], g_ref[...], b_ref[...])


def _outproj_ln(x2, ot, og, wa, wb, g, b, seq, tm):
    n = x2.shape[0]
    nt = seq // tm
    row = lambda width: pl.BlockSpec((tm, width), lambda i: (i, 0))
    return pl.pallas_call(
        _outproj_ln_kernel,
        grid=(n // tm,),
        in_specs=[row(D_MODEL),
                  pl.BlockSpec((1, ot.shape[1], tm), lambda i: (i // nt, 0, i % nt)),
                  row(GLA_VW),
                  _resident(wa.shape), _resident(wb.shape), _resident(g.shape), _resident(b.shape)],
        out_specs=row(D_MODEL),
        out_shape=jax.ShapeDtypeStruct((n, D_MODEL), F32),
        compiler_params=_params(1),
        name="outproj_ln",
    )(x2, ot, og, wa, wb, g, b)


def _sgu_ln_kernel(x_ref, win_ref, lng_ref, lnb_ref, ws_ref, bs_ref, wout_ref, g_ref, b_ref,
                   o_ref, v_ref, vn_ref, p_ref, z_ref):
    i = pl.program_id(0)

    @pl.when(i == 0)
    def _():
        z_ref[1] = jnp.zeros(z_ref.shape[1:], F32)

    prev = (i + 1) % 2
    x = x_ref[...]
    xb = x.astype(BF16)
    tm = x.shape[0]
    n_col = SGU_HALF // SGU_COLS
    ln_rows = tm // n_col
    for jc in range(n_col):
        lo = SGU_HALF + jc * SGU_COLS
        v_ref[:, jc * SGU_COLS:(jc + 1) * SGU_COLS] = _gelu(_dot(xb, win_ref[:, lo:lo + SGU_COLS]))
    def u_chunk(jc):
        hu = _dot(xb, win_ref[:, jc * SGU_COLS:(jc + 1) * SGU_COLS])
        r0 = jc * ln_rows
        _layer_norm_rows(o_ref.at[r0:r0 + ln_rows, :],
                         lambda r: z_ref[prev, r0 + r.start:r0 + r.stop, :],
                         g_ref[...], b_ref[...])
        return _gelu(hu + _zero_after_store(o_ref, i))

    u_next = u_chunk(0)
    lng = lng_ref[...]
    lnb = lnb_ref[...]
    for r in range(0, tm, SGU_LN_ROWS):
        vn_ref[r:r + SGU_LN_ROWS, :] = _layer_norm(v_ref[r:r + SGU_LN_ROWS, :], lng, lnb).astype(BF16)

    for jc in range(n_col):
        u = u_next
        if jc + 1 < n_col:
            u_next = u_chunk(jc + 1)
        for gg in range(SGU_COLS // SGU_GROUP_DIM):
            grp = jc * (SGU_COLS // SGU_GROUP_DIM) + gg
            c0 = grp * SGU_GROUP_DIM
            bias = jnp.concatenate([bs_ref[grp]] * (SGU_GROUP_DIM // LANE), axis=1)
            for n in range(tm // SGU_CHUNK):
                t0 = n * SGU_CHUNK
                mixed = _dot(ws_ref[grp], vn_ref[t0:t0 + SGU_CHUNK, c0:c0 + SGU_GROUP_DIM]) + bias
                ug = u[t0:t0 + SGU_CHUNK, gg * SGU_GROUP_DIM:(gg + 1) * SGU_GROUP_DIM]
                p_ref[t0:t0 + SGU_CHUNK, c0:c0 + SGU_GROUP_DIM] = (ug * mixed).astype(BF16)
    y = _dot(p_ref[...], wout_ref[...])
    z_ref[i % 2] = DN_ALPHA * x + y


def _sgu_ln(x2, win, lng, lnb, ws, bs, wout, g, b, tm):
    n = x2.shape[0]
    nt = n // tm
    row = pl.BlockSpec((tm, D_MODEL), lambda i: (jnp.minimum(i, nt - 1), 0))
    return pl.pallas_call(
        _sgu_ln_kernel,
        grid=(nt + 1,),
        in_specs=[row, _resident(win.shape), _resident(lng.shape), _resident(lnb.shape),
                  _resident(ws.shape), _resident(bs.shape), _resident(wout.shape),
                  _resident(g.shape), _resident(b.shape)],
        out_specs=pl.BlockSpec((tm, D_MODEL), lambda i: (jnp.maximum(i - 1, 0), 0)),
        out_shape=jax.ShapeDtypeStruct((n, D_MODEL), F32),
        scratch_shapes=[pltpu.VMEM((tm, SGU_HALF), F32),
                        pltpu.VMEM((tm, SGU_HALF), BF16),
                        pltpu.VMEM((tm, SGU_HALF), BF16),
                        pltpu.VMEM((2, tm, D_MODEL), F32)],
        compiler_params=_params(1),
        name="sgu_ln",
    )(x2, win, lng, lnb, ws, bs, wout, g, b)


def _row(v):
    return v.reshape(1, -1).astype(F32)


def _pack_ffn(w_gu, w_down):
    return w_gu.astype(BF16), w_down.astype(BF16)


def _rope_tab(seq):
    inv_freq = 1.0 / (ROPE_BASE ** (jnp.arange(0, MLA_ROPE, 2, dtype=F32) / MLA_ROPE))
    ang = jnp.arange(seq, dtype=F32)[:, None] * inv_freq[None, :]
    cos, sin = jnp.cos(ang), jnp.sin(ang)
    pad_lo = jnp.zeros((seq, MLA_NOPE), F32)
    pad_hi = jnp.zeros((seq, MLA_HEAD_PAD - MLA_NOPE - MLA_ROPE), F32)
    k_dir = jnp.concatenate([pad_lo, cos, cos, pad_hi], axis=1)
    k_swp = jnp.concatenate([pad_lo, -sin, sin, pad_hi], axis=1)
    q_scale = (MLA_NOPE + MLA_ROPE) ** -0.5 * math.log2(math.e)
    q_dir = jnp.concatenate([jnp.ones((seq, MLA_NOPE), F32), cos, cos, pad_hi], axis=1) * q_scale
    q_swp = k_swp * q_scale
    return jnp.stack([q_dir, q_swp, k_dir, k_swp])


def _pack_even(w_in, q_norm, w_uq, kv_norm, w_ukv, wg_f, bg_f, wg_b, bg_b, seqs):
    o = 0
    parts = {}
    for name, width in (("cq", MLA_Q_RANK), ("ckv", MLA_KV_RANK), ("kr", MLA_ROPE),
                        ("q", GLA_QK), ("k", GLA_QK), ("v", GLA_VW), ("r", GLA_VW),
                        ("zf", GLA_GATE_RANK), ("zb", GLA_GATE_RANK)):
        parts[name] = w_in[:, o:o + width]
        o += width
    zcol = lambda n: jnp.zeros((D_MODEL, n), F32)
    m1 = jnp.concatenate([parts["zf"], parts["zb"], zcol(MLA_NOPE - 2 * GLA_GATE_RANK),
                          parts["kr"], zcol(MLA_HEAD_PAD - MLA_NOPE - MLA_ROPE)], axis=1)
    win = jnp.concatenate([parts["cq"], parts["ckv"], parts["q"] * (GLA_DK ** -0.5), parts["k"],
                           parts["v"], parts["r"], m1], axis=1).astype(BF16)

    qk = MLA_NOPE + MLA_ROPE
    uq = w_uq.reshape(MLA_Q_RANK, MLA_HEADS, qk)
    pad = MLA_HEAD_PAD - qk
    wqa = jnp.pad(uq, ((0, 0), (0, 0), (0, pad))).reshape(MLA_Q_RANK, -1).astype(BF16)

    ukv = w_ukv.reshape(MLA_KV_RANK, MLA_HEADS, MLA_NOPE + MLA_V)
    wka = jnp.pad(ukv[:, :, :MLA_NOPE], ((0, 0), (0, 0), (0, MLA_HEAD_PAD - MLA_NOPE)))
    wka = wka.reshape(MLA_KV_RANK, -1).astype(BF16)
    wvt = ukv[:, :, MLA_NOPE:].reshape(MLA_KV_RANK, -1).T.astype(BF16)

    gpad = lambda w, at: jnp.pad(w, ((at, LANE - at - GLA_GATE_RANK), (0, 0)))
    wg = jnp.stack([gpad(wg_f, 0), gpad(wg_b, GLA_GATE_RANK)]).astype(BF16)
    bg = jnp.stack([_row(bg_f), _row(bg_b)])
    return dict(win=win, q_norm=_row(q_norm), wqa=wqa, kv_norm=_row(kv_norm),
                wka=wka, wvt=wvt, wg=wg, bg=bg,
                rope_tab={s: _rope_tab(s) for s in seqs})


def _tile(n, pref):
    return pref if n % pref == 0 else n


def kernel(x_prompt, x_sample, l0_ffa_w_gu, l0_ffa_w_down, l0_ln1_g, l0_ln1_b, l0_w_in, l0_mla_q_norm, l0_mla_w_uq, l0_mla_kv_norm, l0_mla_w_ukv, l0_gla_w_gate_f, l0_gla_b_gate_f, l0_gla_w_gate_b, l0_gla_b_gate_b, l0_gla_norm, l0_w_out, l0_ln2_g, l0_ln2_b, l0_ffb_w_gu, l0_ffb_w_down, l0_ln3_g, l0_ln3_b, l1_ffa_w_gu, l1_ffa_w_down, l1_ln1_g, l1_ln1_b, l1_sgu_w_in, l1_sgu_ln_g, l1_sgu_ln_b, l1_sgu_w_s, l1_sgu_b_s, l1_sgu_w_out, l1_ln2_g, l1_ln2_b, l1_ffb_w_gu, l1_ffb_w_down, l1_ln3_g, l1_ln3_b):
    seqs = sorted({x_prompt.shape[1], x_sample.shape[1]})
    ffn = [
        _pack_ffn(l0_ffa_w_gu, l0_ffa_w_down) + (_row(l0_ln1_g), _row(l0_ln1_b)),
        _pack_ffn(l0_ffb_w_gu, l0_ffb_w_down) + (_row(l0_ln3_g), _row(l0_ln3_b)),
        _pack_ffn(l1_ffa_w_gu, l1_ffa_w_down) + (_row(l1_ln1_g), _row(l1_ln1_b)),
        _pack_ffn(l1_ffb_w_gu, l1_ffb_w_down) + (_row(l1_ln3_g), _row(l1_ln3_b)),
    ]
    even = _pack_even(l0_w_in, l0_mla_q_norm, l0_mla_w_uq, l0_mla_kv_norm, l0_mla_w_ukv,
                      l0_gla_w_gate_f, l0_gla_b_gate_f, l0_gla_w_gate_b, l0_gla_b_gate_b, seqs)
    hv = MLA_HEADS * MLA_V
    w_out_a = l0_w_out[:hv].astype(BF16)
    w_out_b = l0_w_out[hv:].astype(BF16)
    gla_norm = _row(l0_gla_norm)
    sgu_bs = jnp.broadcast_to(l1_sgu_b_s[:, :, None], (SGU_GROUPS, SGU_CHUNK, LANE)).astype(F32)
    sgu = (l1_sgu_w_in.astype(BF16), _row(l1_sgu_ln_g), _row(l1_sgu_ln_b),
           l1_sgu_w_s.astype(BF16), sgu_bs, l1_sgu_w_out.astype(BF16),
           _row(l1_ln2_g), _row(l1_ln2_b))

    def trunk(x):
        batch, seq, _ = x.shape
        tm = _tile(seq, 512)
        x2 = x.reshape(batch * seq, D_MODEL)
        x2 = _ffn_ln(x2, *ffn[0], tm)
        q, k, vt, gq, gk, gv, gr, la = _proj_ab(x2, batch, seq, even, tm)
        hp = MLA_HEADS * MLA_HEAD_PAD
        ot = _mla_attn(q.reshape(batch, seq, hp), k.reshape(batch, seq, hp), vt, _tile(seq, 512))
        og = _gla(gq, gk, gv, gr, la, gla_norm, batch, seq, _tile(seq, 1024))
        x2 = _outproj_ln(x2, ot, og, w_out_a, w_out_b, _row(l0_ln2_g), _row(l0_ln2_b), seq, tm)
        x2 = _ffn_ln(x2, *ffn[1], tm)
        x2 = _ffn_ln(x2, *ffn[2], tm)
        x2 = _sgu_ln(x2, *sgu, _tile(seq, 512))
        x2 = _ffn_ln(x2, *ffn[3], tm)
        return x2.reshape(batch, seq, D_MODEL)

    return (trunk(x_prompt), trunk(x_sample))
```

```python
import functools
import math

import jax
import jax.numpy as jnp
from jax import lax
from jax.experimental import pallas as pl
from jax.experimental.pallas import tpu as pltpu

F32 = jnp.float32
BF16 = jnp.bfloat16

D_MODEL = 1024
DEPTH = 2
DN_ALPHA = (2 * DEPTH) ** 0.25
LN_EPS = 1e-5
RMS_EPS = 1e-6

MLA_HEADS = 8
MLA_Q_RANK = 768
MLA_KV_RANK = 256
MLA_NOPE = 64
MLA_ROPE = 32
MLA_V = 64
MLA_HEAD_PAD = 128
ROPE_BASE = 10000.0
MLA_KEY_CHUNK = 1024
MLA_Q_SUB = 256
MLA_ONES_ROWS = 16

GLA_HEADS = 4
GLA_DK = 64
GLA_DV = 128
GLA_GATE_RANK = 16
GLA_TAU = 16.0
GLA_CHUNK = 64
GLA_GROUP = 256
GLA_QK = GLA_HEADS * GLA_DK
GLA_VW = GLA_HEADS * GLA_DV

SGU_CHUNK = 128
SGU_HALF = 3072
SGU_GROUPS = 8
SGU_GROUP_DIM = SGU_HALF // SGU_GROUPS
SGU_COLS = 2 * SGU_GROUP_DIM
SGU_LN_ROWS = 16

FFN_HIDDEN = 2816
FFN_CHUNK = 256

LANE = 128
LN_ROWS = 32
VMEM_LIMIT_BYTES = 56 * 1024 * 1024

_P_CQ = 0
_P_CKV = _P_CQ + MLA_Q_RANK
_P_GQ = _P_CKV + MLA_KV_RANK
_P_GK = _P_GQ + GLA_QK
_P_GV = _P_GK + GLA_QK
_P_GR = _P_GV + GLA_VW
_P_M1 = _P_GR + GLA_VW
_P_END = _P_M1 + LANE


def _resident(shape):
    zeros = (0,) * len(shape)
    return pl.BlockSpec(shape, lambda *_: zeros, pipeline_mode=pl.Buffered(1))


def _params(n_axes):
    return pltpu.CompilerParams(
        dimension_semantics=("arbitrary",) * n_axes,
        vmem_limit_bytes=VMEM_LIMIT_BYTES,
    )


def _layer_norm(y, g, b):
    mu = jnp.mean(y, axis=-1, keepdims=True)
    yc = y - mu
    var = jnp.mean(yc * yc, axis=-1, keepdims=True)
    return yc * lax.rsqrt(var + LN_EPS) * g + b


def _layer_norm_rows(o_ref, load, g, b):
    for r in range(0, o_ref.shape[0], LN_ROWS):
        rows = slice(r, r + LN_ROWS)
        o_ref[rows, :] = _layer_norm(load(rows), g, b)


def _rms_norm(y, g):
    ms = jnp.mean(y * y, axis=-1, keepdims=True)
    return y * lax.rsqrt(ms + RMS_EPS) * g


def _gelu(x):
    return 0.5 * x * (1.0 + lax.erf(x * math.sqrt(0.5)))


def _silu(x):
    return x * jax.nn.sigmoid(x)


def _log_sigmoid(z):
    return jnp.minimum(z, 0.0) - jnp.log1p(jnp.exp(-jnp.abs(z)))


def _zero_after_store(ref, step):
    row0 = pl.multiple_of(jnp.minimum(step, 0), 8)
    words = pltpu.bitcast(ref[pl.ds(row0, 8), :LANE], jnp.uint32)
    zero = pltpu.bitcast((words >> 16) >> 16, F32)
    return zero[:1, :1]


def _dot(a, b):
    return jnp.dot(a, b, preferred_element_type=F32)


def _dot_nt(a, b):
    return lax.dot_general(a, b, (((1,), (1,)), ((), ())), preferred_element_type=F32)


def _dot_tn(a, b):
    return lax.dot_general(a, b, (((0,), (0,)), ((), ())), preferred_element_type=F32)


def _ffn_ln_kernel(x_ref, wgu_ref, wd_ref, g_ref, b_ref, o_ref, a_ref, z_ref):
    i = pl.program_id(0)

    @pl.when(i == 0)
    def _():
        z_ref[1] = jnp.zeros(z_ref.shape[1:], F32)

    prev = (i + 1) % 2
    tm = o_ref.shape[0]
    n_chunks = FFN_HIDDEN // FFN_CHUNK
    ln_chunks = n_chunks - 3
    ln_rows = tm // ln_chunks if tm % ln_chunks == 0 else tm
    last = pl.num_programs(0) - 1

    @pl.when(i < last)
    def _():
        xb = x_ref[...].astype(BF16)
        for c in range(n_chunks):
            lo = c * FFN_CHUNK
            gate = _dot(xb, wgu_ref[:, lo:lo + FFN_CHUNK])
            up = _dot(xb, wgu_ref[:, FFN_HIDDEN + lo:FFN_HIDDEN + lo + FFN_CHUNK])
            if c * ln_rows < tm:
                r0 = c * ln_rows
                _layer_norm_rows(o_ref.at[r0:r0 + ln_rows, :],
                                 lambda r: z_ref[prev, r0 + r.start:r0 + r.stop, :],
                                 g_ref[...], b_ref[...])
                up = up + _zero_after_store(o_ref, i)
            a_ref[:, lo:lo + FFN_CHUNK] = (_silu(gate) * up).astype(BF16)
        y = _dot(a_ref[...], wd_ref[...])
        z_ref[i % 2] = DN_ALPHA * x_ref[...] + 0.5 * y

    @pl.when(i == last)
    def _():
        _layer_norm_rows(o_ref, lambda r: z_ref[prev, r, :], g_ref[...], b_ref[...])


def _ffn_ln(x2, wgu, wd, g, b, tm):
    n = x2.shape[0]
    nt = n // tm
    return pl.pallas_call(
        _ffn_ln_kernel,
        grid=(nt + 1,),
        in_specs=[pl.BlockSpec((tm, D_MODEL), lambda i: (jnp.minimum(i, nt - 1), 0)),
                  _resident(wgu.shape), _resident(wd.shape),
                  _resident(g.shape), _resident(b.shape)],
        out_specs=pl.BlockSpec((tm, D_MODEL), lambda i: (jnp.maximum(i - 1, 0), 0)),
        out_shape=jax.ShapeDtypeStruct((n, D_MODEL), F32),
        scratch_shapes=[pltpu.VMEM((tm, FFN_HIDDEN), BF16),
                        pltpu.VMEM((2, tm, D_MODEL), F32)],
        compiler_params=_params(1),
        name="ffn_ln",
    )(x2, wgu, wd, g, b)


def _swap_rope_halves(x):
    n = x.shape[1]
    half = MLA_ROPE // 2
    lane = lax.broadcasted_iota(jnp.int32, x.shape, 1) & (MLA_HEAD_PAD - 1)
    from_above = pltpu.roll(x, n - half, axis=1)
    from_below = pltpu.roll(x, half, axis=1)
    return jnp.where(lane < MLA_NOPE + half, from_above, from_below)


def _proj_ab_kernel(x_ref, win_ref, tab_ref, qn_ref, wqa_ref, kvn_ref, wka_ref,
                    wvt_ref, wg_ref, bg_ref,
                    q_ref, k_ref, vt_ref, gq_ref, gk_ref, gv_ref, gr_ref, la_ref):
    xb = x_ref[...].astype(BF16)
    p = _dot(xb, win_ref[...])
    gq_ref[...] = p[:, _P_GQ:_P_GK]
    gk_ref[...] = p[:, _P_GK:_P_GV]
    gv_ref[...] = p[:, _P_GV:_P_GR].astype(BF16)
    gr_ref[...] = p[:, _P_GR:_P_M1].astype(BF16)
    m1 = p[:, _P_M1:_P_END]

    qn = _rms_norm(p[:, _P_CQ:_P_CKV], qn_ref[...]).astype(BF16)
    cq = jnp.concatenate([tab_ref[0]] * MLA_HEADS, axis=1)
    sq = jnp.concatenate([tab_ref[1]] * MLA_HEADS, axis=1)
    qa = _dot(qn, wqa_ref[...])
    q_ref[...] = (qa * cq + _swap_rope_halves(qa) * sq).astype(BF16)

    kvn = _rms_norm(p[:, _P_CKV:_P_GQ], kvn_ref[...]).astype(BF16)
    k_rope = m1 * tab_ref[2] + _swap_rope_halves(m1) * tab_ref[3]
    k_ref[...] = (_dot(kvn, wka_ref[...])
                  + jnp.concatenate([k_rope] * MLA_HEADS, axis=1)).astype(BF16)
    vt_ref[0] = _dot_nt(wvt_ref[...], kvn).astype(BF16)

    m1b = m1.astype(BF16)
    for d in range(2):
        z = _dot(m1b, wg_ref[d]) + bg_ref[d]
        la_ref[d] = _log_sigmoid(z) * (1.0 / GLA_TAU)


def _proj_ab(x2, batch, seq, w, tm):
    n = x2.shape[0]
    nt = seq // tm
    row = lambda width: pl.BlockSpec((tm, width), lambda i: (i, 0))
    in_specs = [
        row(D_MODEL),
        _resident(w["win"].shape),
        pl.BlockSpec((4, tm, LANE), lambda i: (0, i % nt, 0)),
        _resident(w["q_norm"].shape), _resident(w["wqa"].shape),
        _resident(w["kv_norm"].shape), _resident(w["wka"].shape), _resident(w["wvt"].shape),
        _resident(w["wg"].shape), _resident(w["bg"].shape),
    ]
    hp = MLA_HEADS * MLA_HEAD_PAD
    hv = MLA_HEADS * MLA_V
    out_specs = [
        row(hp), row(hp),
        pl.BlockSpec((1, hv, tm), lambda i: (i // nt, 0, i % nt)),
        row(GLA_QK), row(GLA_QK), row(GLA_VW), row(GLA_VW),
        pl.BlockSpec((2, tm, GLA_QK), lambda i: (0, i, 0)),
    ]
    out_shape = [
        jax.ShapeDtypeStruct((n, hp), BF16), jax.ShapeDtypeStruct((n, hp), BF16),
        jax.ShapeDtypeStruct((batch, hv, seq), BF16),
        jax.ShapeDtypeStruct((n, GLA_QK), F32), jax.ShapeDtypeStruct((n, GLA_QK), F32),
        jax.ShapeDtypeStruct((n, GLA_VW), BF16), jax.ShapeDtypeStruct((n, GLA_VW), BF16),
        jax.ShapeDtypeStruct((2, n, GLA_QK), F32),
    ]
    return pl.pallas_call(
        _proj_ab_kernel,
        grid=(n // tm,),
        in_specs=in_specs,
        out_specs=out_specs,
        out_shape=out_shape,
        compiler_params=_params(1),
        name="proj_ab",
    )(x2, w["win"], w["rope_tab"][seq], w["q_norm"], w["wqa"], w["kv_norm"],
      w["wka"], w["wvt"], w["wg"], w["bg"])


def _mla_attn_kernel(q_ref, k_ref, vt_ref, o_ref, sa_ref, sb_ref):
    seq = k_ref.shape[1]
    ck = min(MLA_KEY_CHUNK, seq)
    n_chunks = seq // ck
    ones = jnp.ones((MLA_ONES_ROWS, ck), BF16)
    s_refs = (sa_ref, sb_ref)
    dyn0 = jnp.minimum(pl.program_id(0), 0)

    tq = q_ref.shape[1]
    qw = min(MLA_Q_SUB, tq)
    units = [(qt, h) for qt in range(tq // qw) for h in range(MLA_HEADS)]

    def qk_chunk(u, c, m):
        qt, h = units[u]
        lanes = slice(h * MLA_HEAD_PAD, (h + 1) * MLA_HEAD_PAD)
        keys = slice(c * ck, (c + 1) * ck)
        s_t = _dot_nt(k_ref[0, keys, lanes], q_ref[0, qt * qw:(qt + 1) * qw, lanes])
        s_refs[u % 2][pl.ds(pl.multiple_of(c * ck + dyn0, ck), ck), :] = s_t
        mc = jnp.max(s_t, axis=0, keepdims=True)
        return mc if m is None else jnp.maximum(m, mc)

    def pv_chunk(u, c, m, acc):
        _, h = units[u]
        keys = slice(c * ck, (c + 1) * ck)
        s_t = s_refs[u % 2][pl.ds(pl.multiple_of(c * ck + dyn0, ck), ck), :]
        p = jnp.exp2((s_t - m).astype(BF16))
        vt_ext = jnp.concatenate([vt_ref[0, h * MLA_V:(h + 1) * MLA_V, keys], ones], axis=0)
        d = _dot(vt_ext, p)
        return d if acc is None else acc + d

    m = [None] * (len(units) + 1)
    for c in range(n_chunks):
        m[0] = qk_chunk(0, c, m[0])
    for u, (qt, h) in enumerate(units):
        acc = None
        for c in range(n_chunks):
            if u + 1 < len(units):
                m[u + 1] = qk_chunk(u + 1, c, m[u + 1])
            acc = pv_chunk(u, c, m[u], acc)
        o_ref[0, h * MLA_V:(h + 1) * MLA_V, qt * qw:(qt + 1) * qw] = (
            acc[:MLA_V] / acc[MLA_V:MLA_V + 1]).astype(BF16)


def _mla_attn(q, k, vt, tq):
    batch, seq, hp = q.shape
    hv = MLA_HEADS * MLA_V
    return pl.pallas_call(
        _mla_attn_kernel,
        grid=(batch, seq // tq),
        in_specs=[
            pl.BlockSpec((1, tq, hp), lambda b, i: (b, i, 0)),
            pl.BlockSpec((1, seq, hp), lambda b, i: (b, 0, 0)),
            pl.BlockSpec((1, hv, seq), lambda b, i: (b, 0, 0)),
        ],
        out_specs=pl.BlockSpec((1, hv, tq), lambda b, i: (b, 0, i)),
        out_shape=jax.ShapeDtypeStruct((batch, hv, seq), BF16),
        scratch_shapes=[pltpu.VMEM((seq, min(MLA_Q_SUB, tq)), F32)] * 2,
        compiler_params=_params(2),
        name="mla_attn",
    )(q, k, vt)


def _split2(x):
    hi = x.astype(BF16)
    mid = (x - hi.astype(F32)).astype(BF16)
    return hi, mid


def _gla_kernel(q_ref, k_ref, v_ref, r_ref, la_ref, ng_ref, o_ref,
                state_ref, ofwd_ref, oblk_ref, dstate_ref, decay_ref, states_ref,
                *, n_blocks, tb):
    phase = pl.program_id(1)
    j = pl.program_id(2)
    is_fwd = phase == 0
    n_chunks = tb // GLA_CHUNK
    C = GLA_CHUNK

    @pl.when(j == 0)
    def _():
        state_ref[...] = jnp.zeros_like(state_ref)

    G = min(GLA_GROUP, tb)
    cpg = G // C
    n_groups = tb // G
    H = GLA_HEADS

    sign = jnp.where(is_fwd, 1, -1)
    far = 4 * G

    def rel_matrix(rows):
        t = lax.broadcasted_iota(jnp.int32, (rows, G), 0) & (G - 1)
        s = lax.broadcasted_iota(jnp.int32, (rows, G), 1)
        same_chunk = ((t ^ s) & -C) == 0
        return jnp.where(same_chunk, (s - t) * sign, far)

    cum_mat = jnp.where(rel_matrix(G) <= 0, 1.0, 0.0).astype(BF16)
    att_keep = rel_matrix(H * G) <= jnp.where(is_fwd, 0, -1)
    chunk_rows = (lax.broadcasted_iota(jnp.int32, (cpg * GLA_DK, G), 0) // GLA_DK
                  == lax.broadcasted_iota(jnp.int32, (cpg * GLA_DK, G), 1) // C)
    lane_head_g = lax.broadcasted_iota(jnp.int32, (G, GLA_QK), 1) // GLA_DK
    lane_head_c = lax.broadcasted_iota(jnp.int32, (C, GLA_QK), 1) // GLA_DK

    groups = range(n_groups)
    rows = [slice(g * G, (g + 1) * G) for g in groups]
    las = [la_ref[0, rows[g], :] for g in groups]
    splits = [_split2(la) for la in las]
    bs = [_dot(cum_mat, hi) + _dot(cum_mat, mid) for hi, mid in splits]
    tots = [[jnp.sum(la[c * C:(c + 1) * C], axis=0, keepdims=True) for c in range(cpg)]
            for la in las]
    b_tots = [jnp.concatenate([jnp.broadcast_to(t, (C, GLA_QK)) for t in tots[g]], axis=0)
              for g in groups]
    q_ins = [(q_ref[rows[g], :] * jnp.exp(bs[g])).astype(BF16) for g in groups]
    k_ins = [(k_ref[rows[g], :] * jnp.exp(-bs[g])).astype(BF16) for g in groups]
    k_st_ts = [jnp.transpose(k_ref[rows[g], :] * jnp.exp(b_tots[g] - bs[g])) for g in groups]
    q_stacks = [jnp.concatenate(
        [jnp.where(lane_head_g == h, q_ins[g], jnp.zeros_like(q_ins[g])) for h in range(H)],
        axis=0) for g in groups]
    atts = [jnp.where(att_keep, _dot_nt(q_stacks[g], k_ins[g]), 0.0).astype(BF16)
            for g in groups]
    for g in groups:
        for h in range(H):
            k_t = k_st_ts[g][h * GLA_DK:(h + 1) * GLA_DK, :]
            lhs = jnp.where(chunk_rows, jnp.concatenate([k_t] * cpg, axis=0), 0.0).astype(BF16)
            d_h = _dot(lhs, v_ref[rows[g], h * GLA_DV:(h + 1) * GLA_DV])
            for c in range(cpg):
                dstate_ref[g * cpg + c, h * GLA_DK:(h + 1) * GLA_DK, :] = (
                    d_h[c * GLA_DK:(c + 1) * GLA_DK])
        for c in range(cpg):
            decay_ref[g * cpg + c] = jnp.exp(
                jnp.transpose(jnp.broadcast_to(tots[g][c], (GLA_DV, GLA_QK))))
    o_intras = [jnp.concatenate(
        [_dot(atts[g][h * G:(h + 1) * G], v_ref[rows[g], h * GLA_DV:(h + 1) * GLA_DV])
         for h in range(H)], axis=1) for g in groups]
    q_in_chunks = [q_ins[g][c * C:(c + 1) * C] for g in groups for c in range(cpg)]
    o_intra_chunks = [o_intras[g][c * C:(c + 1) * C] for g in groups for c in range(cpg)]

    state = state_ref[...]
    for i in range(n_chunks):
        ch = jnp.where(is_fwd, i, n_chunks - 1 - i)
        states_ref[ch] = state
        state = decay_ref[ch] * state + dstate_ref[ch]
    state_ref[...] = state

    for ch in range(n_chunks):
        q_stack = jnp.concatenate(
            [jnp.where(lane_head_c == h, q_in_chunks[ch], jnp.zeros_like(q_in_chunks[ch]))
             for h in range(H)], axis=0)
        inter = _dot(q_stack, states_ref[ch].astype(BF16))
        oblk_ref[ch * C:(ch + 1) * C, :] = o_intra_chunks[ch] + jnp.concatenate(
            [inter[h * C:(h + 1) * C] for h in range(H)], axis=1)

    blk = jnp.where(is_fwd, j, n_blocks - 1 - j)
    row0 = pl.multiple_of(blk * tb, tb)

    @pl.when(is_fwd)
    def _():
        ofwd_ref[pl.ds(row0, tb), :] = oblk_ref[...]

    @pl.when(jnp.logical_not(is_fwd))
    def _():
        ng = ng_ref[...]
        for r in range(0, tb, C):
            o = ofwd_ref[pl.ds(pl.multiple_of(row0 + r, C), C), :] + oblk_ref[r:r + C, :]
            normed = jnp.concatenate(
                [_rms_norm(o[:, h * GLA_DV:(h + 1) * GLA_DV], ng) for h in range(H)], axis=1)
            o_ref[r:r + C, :] = (normed * _silu(r_ref[r:r + C, :].astype(F32))).astype(BF16)


def _gla(gq, gk, gv, gr, la, norm_g, batch, seq, tb):
    nb = seq // tb

    def blk(ph, j):
        return jnp.where(ph == 0, j, nb - 1 - j)

    def row(width):
        return pl.BlockSpec((tb, width), lambda b, ph, j: (b * nb + blk(ph, j), 0))

    out_spec = pl.BlockSpec((tb, GLA_VW), lambda b, ph, j: (b * nb + nb - 1 - ph * j, 0))
    kern = functools.partial(_gla_kernel, n_blocks=nb, tb=tb)
    return pl.pallas_call(
        kern,
        grid=(batch, 2, nb),
        in_specs=[row(GLA_QK), row(GLA_QK), row(GLA_VW), row(GLA_VW),
                  pl.BlockSpec((1, tb, GLA_QK), lambda b, ph, j: (ph, b * nb + blk(ph, j), 0)),
                  _resident(norm_g.shape)],
        out_specs=out_spec,
        out_shape=jax.ShapeDtypeStruct((batch * seq, GLA_VW), BF16),
        scratch_shapes=[pltpu.VMEM((GLA_QK, GLA_DV), F32),
                        pltpu.VMEM((seq, GLA_VW), F32),
                        pltpu.VMEM((tb, GLA_VW), F32),
                        pltpu.VMEM((tb // GLA_CHUNK, GLA_QK, GLA_DV), F32),
                        pltpu.VMEM((tb // GLA_CHUNK, GLA_QK, GLA_DV), F32),
                        pltpu.VMEM((tb // GLA_CHUNK, GLA_QK, GLA_DV), F32)],
        compiler_params=_params(3),
        name="gla",
    )(gq, gk, gv, gr, la, norm_g)


def _outproj_ln_kernel(x_ref, ot_ref, og_ref, wa_ref, wb_ref, g_ref, b_ref, o_ref):
    y = _dot_tn(ot_ref[0], wa_ref[...]) + _dot(og_ref[...], wb_ref[...])
    _layer_norm_rows(o_ref, lambda r: DN_ALPHA * x_ref[r, :] + y[r, :], g_ref[...], b_ref[...])


def _outproj_ln(x2, ot, og, wa, wb, g, b, seq, tm):
    n = x2.shape[0]
    nt = seq // tm
    row = lambda width: pl.BlockSpec((tm, width), lambda i: (i, 0))
    return pl.pallas_call(
        _outproj_ln_kernel,
        grid=(n // tm,),
        in_specs=[row(D_MODEL),
                  pl.BlockSpec((1, ot.shape[1], tm), lambda i: (i // nt, 0, i % nt)),
                  row(GLA_VW),
                  _resident(wa.shape), _resident(wb.shape), _resident(g.shape), _resident(b.shape)],
        out_specs=row(D_MODEL),
        out_shape=jax.ShapeDtypeStruct((n, D_MODEL), F32),
        compiler_params=_params(1),
        name="outproj_ln",
    )(x2, ot, og, wa, wb, g, b)


def _sgu_ln_kernel(x_ref, win_ref, lng_ref, lnb_ref, ws_ref, bs_ref, wout_ref, g_ref, b_ref,
                   o_ref, v_ref, vn_ref, p_ref, z_ref):
    i = pl.program_id(0)

    @pl.when(i == 0)
    def _():
        z_ref[1] = jnp.zeros(z_ref.shape[1:], F32)

    prev = (i + 1) % 2
    x = x_ref[...]
    xb = x.astype(BF16)
    tm = x.shape[0]
    n_col = SGU_HALF // SGU_COLS
    ln_rows = tm // n_col
    for jc in range(n_col):
        lo = SGU_HALF + jc * SGU_COLS
        v_ref[:, jc * SGU_COLS:(jc + 1) * SGU_COLS] = _gelu(_dot(xb, win_ref[:, lo:lo + SGU_COLS]))
    def u_chunk(jc):
        hu = _dot(xb, win_ref[:, jc * SGU_COLS:(jc + 1) * SGU_COLS])
        r0 = jc * ln_rows
        _layer_norm_rows(o_ref.at[r0:r0 + ln_rows, :],
                         lambda r: z_ref[prev, r0 + r.start:r0 + r.stop, :],
                         g_ref[...], b_ref[...])
        return _gelu(hu + _zero_after_store(o_ref, i))

    u_next = u_chunk(0)
    lng = lng_ref[...]
    lnb = lnb_ref[...]
    for r in range(0, tm, SGU_LN_ROWS):
        vn_ref[r:r + SGU_LN_ROWS, :] = _layer_norm(v_ref[r:r + SGU_LN_ROWS, :], lng, lnb).astype(BF16)

    for jc in range(n_col):
        u = u_next
        if jc + 1 < n_col:
            u_next = u_chunk(jc + 1)
        for gg in range(SGU_COLS // SGU_GROUP_DIM):
            grp = jc * (SGU_COLS // SGU_GROUP_DIM) + gg
            c0 = grp * SGU_GROUP_DIM
            bias = jnp.concatenate([bs_ref[grp]] * (SGU_GROUP_DIM // LANE), axis=1)
            for n in range(tm // SGU_CHUNK):
                t0 = n * SGU_CHUNK
                mixed = _dot(ws_ref[grp], vn_ref[t0:t0 + SGU_CHUNK, c0:c0 + SGU_GROUP_DIM]) + bias
                ug = u[t0:t0 + SGU_CHUNK, gg * SGU_GROUP_DIM:(gg + 1) * SGU_GROUP_DIM]
                p_ref[t0:t0 + SGU_CHUNK, c0:c0 + SGU_GROUP_DIM] = (ug * mixed).astype(BF16)
    y = _dot(p_ref[...], wout_ref[...])
    z_ref[i % 2] = DN_ALPHA * x + y


def _sgu_ln(x2, win, lng, lnb, ws, bs, wout, g, b, tm):
    n = x2.shape[0]
    nt = n // tm
    row = pl.BlockSpec((tm, D_MODEL), lambda i: (jnp.minimum(i, nt - 1), 0))
    return pl.pallas_call(
        _sgu_ln_kernel,
        grid=(nt + 1,),
        in_specs=[row, _resident(win.shape), _resident(lng.shape), _resident(lnb.shape),
                  _resident(ws.shape), _resident(bs.shape), _resident(wout.shape),
                  _resident(g.shape), _resident(b.shape)],
        out_specs=pl.BlockSpec((tm, D_MODEL), lambda i: (jnp.maximum(i - 1, 0), 0)),
        out_shape=jax.ShapeDtypeStruct((n, D_MODEL), F32),
        scratch_shapes=[pltpu.VMEM((tm, SGU_HALF), F32),
                        pltpu.VMEM((tm, SGU_HALF), BF16),
                        pltpu.VMEM((tm, SGU_HALF), BF16),
                        pltpu.VMEM((2, tm, D_MODEL), F32)],
        compiler_params=_params(1),
        name="sgu_ln",
    )(x2, win, lng, lnb, ws, bs, wout, g, b)


def _row(v):
    return v.reshape(1, -1).astype(F32)


def _pack_ffn(w_gu, w_down):
    return w_gu.astype(BF16), w_down.astype(BF16)


def _rope_tab(seq):
    inv_freq = 1.0 / (ROPE_BASE ** (jnp.arange(0, MLA_ROPE, 2, dtype=F32) / MLA_ROPE))
    ang = jnp.arange(seq, dtype=F32)[:, None] * inv_freq[None, :]
    cos, sin = jnp.cos(ang), jnp.sin(ang)
    pad_lo = jnp.zeros((seq, MLA_NOPE), F32)
    pad_hi = jnp.zeros((seq, MLA_HEAD_PAD - MLA_NOPE - MLA_ROPE), F32)
    k_dir = jnp.concatenate([pad_lo, cos, cos, pad_hi], axis=1)
    k_swp = jnp.concatenate([pad_lo, -sin, sin, pad_hi], axis=1)
    q_scale = (MLA_NOPE + MLA_ROPE) ** -0.5 * math.log2(math.e)
    q_dir = jnp.concatenate([jnp.ones((seq, MLA_NOPE), F32), cos, cos, pad_hi], axis=1) * q_scale
    q_swp = k_swp * q_scale
    return jnp.stack([q_dir, q_swp, k_dir, k_swp])


def _pack_even(w_in, q_norm, w_uq, kv_norm, w_ukv, wg_f, bg_f, wg_b, bg_b, seqs):
    o = 0
    parts = {}
    for name, width in (("cq", MLA_Q_RANK), ("ckv", MLA_KV_RANK), ("kr", MLA_ROPE),
                        ("q", GLA_QK), ("k", GLA_QK), ("v", GLA_VW), ("r", GLA_VW),
                        ("zf", GLA_GATE_RANK), ("zb", GLA_GATE_RANK)):
        parts[name] = w_in[:, o:o + width]
        o += width
    zcol = lambda n: jnp.zeros((D_MODEL, n), F32)
    m1 = jnp.concatenate([parts["zf"], parts["zb"], zcol(MLA_NOPE - 2 * GLA_GATE_RANK),
                          parts["kr"], zcol(MLA_HEAD_PAD - MLA_NOPE - MLA_ROPE)], axis=1)
    win = jnp.concatenate([parts["cq"], parts["ckv"], parts["q"] * (GLA_DK ** -0.5), parts["k"],
                           parts["v"], parts["r"], m1], axis=1).astype(BF16)

    qk = MLA_NOPE + MLA_ROPE
    uq = w_uq.reshape(MLA_Q_RANK, MLA_HEADS, qk)
    pad = MLA_HEAD_PAD - qk
    wqa = jnp.pad(uq, ((0, 0), (0, 0), (0, pad))).reshape(MLA_Q_RANK, -1).astype(BF16)

    ukv = w_ukv.reshape(MLA_KV_RANK, MLA_HEADS, MLA_NOPE + MLA_V)
    wka = jnp.pad(ukv[:, :, :MLA_NOPE], ((0, 0), (0, 0), (0, MLA_HEAD_PAD - MLA_NOPE)))
    wka = wka.reshape(MLA_KV_RANK, -1).astype(BF16)
    wvt = ukv[:, :, MLA_NOPE:].reshape(MLA_KV_RANK, -1).T.astype(BF16)

    gpad = lambda w, at: jnp.pad(w, ((at, LANE - at - GLA_GATE_RANK), (0, 0)))
    wg = jnp.stack([gpad(wg_f, 0), gpad(wg_b, GLA_GATE_RANK)]).astype(BF16)
    bg = jnp.stack([_row(bg_f), _row(bg_b)])
    return dict(win=win, q_norm=_row(q_norm), wqa=wqa, kv_norm=_row(kv_norm),
                wka=wka, wvt=wvt, wg=wg, bg=bg,
                rope_tab={s: _rope_tab(s) for s in seqs})


def _tile(n, pref):
    return pref if n % pref == 0 else n


def kernel(x_prompt, x_sample, l0_ffa_w_gu, l0_ffa_w_down, l0_ln1_g, l0_ln1_b, l0_w_in, l0_mla_q_norm, l0_mla_w_uq, l0_mla_kv_norm, l0_mla_w_ukv, l0_gla_w_gate_f, l0_gla_b_gate_f, l0_gla_w_gate_b, l0_gla_b_gate_b, l0_gla_norm, l0_w_out, l0_ln2_g, l0_ln2_b, l0_ffb_w_gu, l0_ffb_w_down, l0_ln3_g, l0_ln3_b, l1_ffa_w_gu, l1_ffa_w_down, l1_ln1_g, l1_ln1_b, l1_sgu_w_in, l1_sgu_ln_g, l1_sgu_ln_b, l1_sgu_w_s, l1_sgu_b_s, l1_sgu_w_out, l1_ln2_g, l1_ln2_b, l1_ffb_w_gu, l1_ffb_w_down, l1_ln3_g, l1_ln3_b):
    seqs = sorted({x_prompt.shape[1], x_sample.shape[1]})
    ffn = [
        _pack_ffn(l0_ffa_w_gu, l0_ffa_w_down) + (_row(l0_ln1_g), _row(l0_ln1_b)),
        _pack_ffn(l0_ffb_w_gu, l0_ffb_w_down) + (_row(l0_ln3_g), _row(l0_ln3_b)),
        _pack_ffn(l1_ffa_w_gu, l1_ffa_w_down) + (_row(l1_ln1_g), _row(l1_ln1_b)),
        _pack_ffn(l1_ffb_w_gu, l1_ffb_w_down) + (_row(l1_ln3_g), _row(l1_ln3_b)),
    ]
    even = _pack_even(l0_w_in, l0_mla_q_norm, l0_mla_w_uq, l0_mla_kv_norm, l0_mla_w_ukv,
                      l0_gla_w_gate_f, l0_gla_b_gate_f, l0_gla_w_gate_b, l0_gla_b_gate_b, seqs)
    hv = MLA_HEADS * MLA_V
    w_out_a = l0_w_out[:hv].astype(BF16)
    w_out_b = l0_w_out[hv:].astype(BF16)
    gla_norm = _row(l0_gla_norm)
    sgu_bs = jnp.broadcast_to(l1_sgu_b_s[:, :, None], (SGU_GROUPS, SGU_CHUNK, LANE)).astype(F32)
    sgu = (l1_sgu_w_in.astype(BF16), _row(l1_sgu_ln_g), _row(l1_sgu_ln_b),
           l1_sgu_w_s.astype(BF16), sgu_bs, l1_sgu_w_out.astype(BF16),
           _row(l1_ln2_g), _row(l1_ln2_b))

    def trunk(x):
        batch, seq, _ = x.shape
        tm = _tile(seq, 512)
        tm_ffn = _tile(seq, 1024)
        x2 = x.reshape(batch * seq, D_MODEL)
        x2 = _ffn_ln(x2, *ffn[0], tm_ffn)
        q, k, vt, gq, gk, gv, gr, la = _proj_ab(x2, batch, seq, even, tm_ffn)
        hp = MLA_HEADS * MLA_HEAD_PAD
        ot = _mla_attn(q.reshape(batch, seq, hp), k.reshape(batch, seq, hp), vt, _tile(seq, 512))
        og = _gla(gq, gk, gv, gr, la, gla_norm, batch, seq, _tile(seq, 1024))
        x2 = _outproj_ln(x2, ot, og, w_out_a, w_out_b, _row(l0_ln2_g), _row(l0_ln2_b), seq, tm_ffn)
        x2 = _ffn_ln(x2, *ffn[1], tm_ffn)
        x2 = _ffn_ln(x2, *ffn[2], tm_ffn)
        x2 = _sgu_ln(x2, *sgu, _tile(seq, 512))
        x2 = _ffn_ln(x2, *ffn[3], tm_ffn)
        return x2.reshape(batch, seq, D_MODEL)

    return (trunk(x_prompt), trunk(x_sample))
```

```python
import functools
import math

import jax
import jax.numpy as jnp
from jax import lax
from jax.experimental import pallas as pl
from jax.experimental.pallas import tpu as pltpu

F32 = jnp.float32
BF16 = jnp.bfloat16

D_MODEL = 1024
DEPTH = 2
DN_ALPHA = (2 * DEPTH) ** 0.25
LN_EPS = 1e-5
RMS_EPS = 1e-6

MLA_HEADS = 8
MLA_Q_RANK = 768
MLA_KV_RANK = 256
MLA_NOPE = 64
MLA_ROPE = 32
MLA_V = 64
MLA_HEAD_PAD = 128
ROPE_BASE = 10000.0
MLA_KEY_CHUNK = 1024
MLA_Q_SUB = 256
MLA_ONES_ROWS = 16

GLA_HEADS = 4
GLA_DK = 64
GLA_DV = 128
GLA_GATE_RANK = 16
GLA_TAU = 16.0
GLA_CHUNK = 64
GLA_GROUP = 256
GLA_QK = GLA_HEADS * GLA_DK
GLA_VW = GLA_HEADS * GLA_DV

SGU_CHUNK = 128
SGU_HALF = 3072
SGU_GROUPS = 8
SGU_GROUP_DIM = SGU_HALF // SGU_GROUPS
SGU_COLS = 2 * SGU_GROUP_DIM
SGU_LN_ROWS = 16

FFN_HIDDEN = 2816
FFN_CHUNK = 256

LANE = 128
LN_ROWS = 32
VMEM_LIMIT_BYTES = 56 * 1024 * 1024

_P_CQ = 0
_P_CKV = _P_CQ + MLA_Q_RANK
_P_GQ = _P_CKV + MLA_KV_RANK
_P_GK = _P_GQ + GLA_QK
_P_GV = _P_GK + GLA_QK
_P_GR = _P_GV + GLA_VW
_P_M1 = _P_GR + GLA_VW
_P_END = _P_M1 + LANE


def _resident(shape):
    zeros = (0,) * len(shape)
    return pl.BlockSpec(shape, lambda *_: zeros, pipeline_mode=pl.Buffered(1))


def _params(n_axes):
    return pltpu.CompilerParams(
        dimension_semantics=("arbitrary",) * n_axes,
        vmem_limit_bytes=VMEM_LIMIT_BYTES,
    )


def _layer_norm(y, g, b):
    mu = jnp.mean(y, axis=-1, keepdims=True)
    yc = y - mu
    var = jnp.mean(yc * yc, axis=-1, keepdims=True)
    return yc * lax.rsqrt(var + LN_EPS) * g + b


def _layer_norm_rows(o_ref, load, g, b):
    for r in range(0, o_ref.shape[0], LN_ROWS):
        rows = slice(r, r + LN_ROWS)
        o_ref[rows, :] = _layer_norm(load(rows), g, b)


def _rms_norm(y, g):
    ms = jnp.mean(y * y, axis=-1, keepdims=True)
    return y * lax.rsqrt(ms + RMS_EPS) * g


def _gelu(x):
    return 0.5 * x * (1.0 + lax.erf(x * math.sqrt(0.5)))


def _silu(x):
    return x * jax.nn.sigmoid(x)


def _log_sigmoid(z):
    return jnp.minimum(z, 0.0) - jnp.log1p(jnp.exp(-jnp.abs(z)))


def _zero_after_store(ref, step):
    row0 = pl.multiple_of(jnp.minimum(step, 0), 8)
    words = pltpu.bitcast(ref[pl.ds(row0, 8), :LANE], jnp.uint32)
    zero = pltpu.bitcast((words >> 16) >> 16, F32)
    return zero[:1, :1]


def _dot(a, b):
    return jnp.dot(a, b, preferred_element_type=F32)


def _dot_nt(a, b):
    return lax.dot_general(a, b, (((1,), (1,)), ((), ())), preferred_element_type=F32)


def _dot_tn(a, b):
    return lax.dot_general(a, b, (((0,), (0,)), ((), ())), preferred_element_type=F32)


def _ffn_ln_kernel(x_ref, wgu_ref, wd_ref, g_ref, b_ref, o_ref, a_ref, z_ref):
    i = pl.program_id(0)

    @pl.when(i == 0)
    def _():
        z_ref[1] = jnp.zeros(z_ref.shape[1:], F32)

    prev = (i + 1) % 2
    tm = o_ref.shape[0]
    n_chunks = FFN_HIDDEN // FFN_CHUNK
    ln_chunks = n_chunks - 3
    ln_rows = tm // ln_chunks if tm % ln_chunks == 0 else tm
    last = pl.num_programs(0) - 1

    @pl.when(i < last)
    def _():
        xb = x_ref[...].astype(BF16)
        for c in range(n_chunks):
            lo = c * FFN_CHUNK
            gate = _dot(xb, wgu_ref[:, lo:lo + FFN_CHUNK])
            up = _dot(xb, wgu_ref[:, FFN_HIDDEN + lo:FFN_HIDDEN + lo + FFN_CHUNK])
            if c * ln_rows < tm:
                r0 = c * ln_rows
                _layer_norm_rows(o_ref.at[r0:r0 + ln_rows, :],
                                 lambda r: z_ref[prev, r0 + r.start:r0 + r.stop, :],
                                 g_ref[...], b_ref[...])
                up = up + _zero_after_store(o_ref, i)
            a_ref[:, lo:lo + FFN_CHUNK] = (_silu(gate) * up).astype(BF16)
        y = _dot(a_ref[...], wd_ref[...])
        z_ref[i % 2] = DN_ALPHA * x_ref[...] + 0.5 * y

    @pl.when(i == last)
    def _():
        _layer_norm_rows(o_ref, lambda r: z_ref[prev, r, :], g_ref[...], b_ref[...])


def _ffn_ln(x2, wgu, wd, g, b, tm):
    n = x2.shape[0]
    nt = n // tm
    return pl.pallas_call(
        _ffn_ln_kernel,
        grid=(nt + 1,),
        in_specs=[pl.BlockSpec((tm, D_MODEL), lambda i: (jnp.minimum(i, nt - 1), 0)),
                  _resident(wgu.shape), _resident(wd.shape),
                  _resident(g.shape), _resident(b.shape)],
        out_specs=pl.BlockSpec((tm, D_MODEL), lambda i: (jnp.maximum(i - 1, 0), 0)),
        out_shape=jax.ShapeDtypeStruct((n, D_MODEL), F32),
        scratch_shapes=[pltpu.VMEM((tm, FFN_HIDDEN), BF16),
                        pltpu.VMEM((2, tm, D_MODEL), F32)],
        compiler_params=_params(1),
        name="ffn_ln",
    )(x2, wgu, wd, g, b)


def _swap_rope_halves(x):
    n = x.shape[1]
    half = MLA_ROPE // 2
    lane = lax.broadcasted_iota(jnp.int32, x.shape, 1) & (MLA_HEAD_PAD - 1)
    from_above = pltpu.roll(x, n - half, axis=1)
    from_below = pltpu.roll(x, half, axis=1)
    return jnp.where(lane < MLA_NOPE + half, from_above, from_below)


def _proj_ab_kernel(x_ref, win_ref, tab_ref, qn_ref, wqa_ref, kvn_ref, wka_ref,
                    wvt_ref, wg_ref, bg_ref,
                    q_ref, k_ref, vt_ref, gq_ref, gk_ref, gv_ref, gr_ref, la_ref):
    xb = x_ref[...].astype(BF16)
    p = _dot(xb, win_ref[...])
    gq_ref[...] = p[:, _P_GQ:_P_GK]
    gk_ref[...] = p[:, _P_GK:_P_GV]
    gv_ref[...] = p[:, _P_GV:_P_GR].astype(BF16)
    gr_ref[...] = p[:, _P_GR:_P_M1].astype(BF16)
    m1 = p[:, _P_M1:_P_END]

    qn = _rms_norm(p[:, _P_CQ:_P_CKV], qn_ref[...]).astype(BF16)
    cq = jnp.concatenate([tab_ref[0]] * MLA_HEADS, axis=1)
    sq = jnp.concatenate([tab_ref[1]] * MLA_HEADS, axis=1)
    qa = _dot(qn, wqa_ref[...])
    q_ref[...] = (qa * cq + _swap_rope_halves(qa) * sq).astype(BF16)

    kvn = _rms_norm(p[:, _P_CKV:_P_GQ], kvn_ref[...]).astype(BF16)
    k_rope = m1 * tab_ref[2] + _swap_rope_halves(m1) * tab_ref[3]
    k_ref[...] = (_dot(kvn, wka_ref[...])
                  + jnp.concatenate([k_rope] * MLA_HEADS, axis=1)).astype(BF16)
    vt_ref[0] = _dot_nt(wvt_ref[...], kvn).astype(BF16)

    m1b = m1.astype(BF16)
    for d in range(2):
        z = _dot(m1b, wg_ref[d]) + bg_ref[d]
        la_ref[d] = _log_sigmoid(z) * (1.0 / GLA_TAU)


def _proj_ab(x2, batch, seq, w, tm):
    n = x2.shape[0]
    nt = seq // tm
    row = lambda width: pl.BlockSpec((tm, width), lambda i: (i, 0))
    in_specs = [
        row(D_MODEL),
        _resident(w["win"].shape),
        pl.BlockSpec((4, tm, LANE), lambda i: (0, i % nt, 0)),
        _resident(w["q_norm"].shape), _resident(w["wqa"].shape),
        _resident(w["kv_norm"].shape), _resident(w["wka"].shape), _resident(w["wvt"].shape),
        _resident(w["wg"].shape), _resident(w["bg"].shape),
    ]
    hp = MLA_HEADS * MLA_HEAD_PAD
    hv = MLA_HEADS * MLA_V
    out_specs = [
        row(hp), row(hp),
        pl.BlockSpec((1, hv, tm), lambda i: (i // nt, 0, i % nt)),
        row(GLA_QK), row(GLA_QK), row(GLA_VW), row(GLA_VW),
        pl.BlockSpec((2, tm, GLA_QK), lambda i: (0, i, 0)),
    ]
    out_shape = [
        jax.ShapeDtypeStruct((n, hp), BF16), jax.ShapeDtypeStruct((n, hp), BF16),
        jax.ShapeDtypeStruct((batch, hv, seq), BF16),
        jax.ShapeDtypeStruct((n, GLA_QK), F32), jax.ShapeDtypeStruct((n, GLA_QK), F32),
        jax.ShapeDtypeStruct((n, GLA_VW), BF16), jax.ShapeDtypeStruct((n, GLA_VW), BF16),
        jax.ShapeDtypeStruct((2, n, GLA_QK), F32),
    ]
    return pl.pallas_call(
        _proj_ab_kernel,
        grid=(n // tm,),
        in_specs=in_specs,
        out_specs=out_specs,
        out_shape=out_shape,
        compiler_params=_params(1),
        name="proj_ab",
    )(x2, w["win"], w["rope_tab"][seq], w["q_norm"], w["wqa"], w["kv_norm"],
      w["wka"], w["wvt"], w["wg"], w["bg"])


def _mla_attn_kernel(q_ref, k_ref, vt_ref, o_ref, sa_ref, sb_ref):
    seq = k_ref.shape[1]
    ck = min(MLA_KEY_CHUNK, seq)
    n_chunks = seq // ck
    ones = jnp.ones((MLA_ONES_ROWS, ck), BF16)
    s_refs = (sa_ref, sb_ref)
    dyn0 = jnp.minimum(pl.program_id(0), 0)

    tq = q_ref.shape[1]
    qw = min(MLA_Q_SUB, tq)
    units = [(qt, h) for qt in range(tq // qw) for h in range(MLA_HEADS)]

    def qk_chunk(u, c, m):
        qt, h = units[u]
        lanes = slice(h * MLA_HEAD_PAD, (h + 1) * MLA_HEAD_PAD)
        keys = slice(c * ck, (c + 1) * ck)
        s_t = _dot_nt(k_ref[0, keys, lanes], q_ref[0, qt * qw:(qt + 1) * qw, lanes])
        s_refs[u % 2][pl.ds(pl.multiple_of(c * ck + dyn0, ck), ck), :] = s_t
        mc = jnp.max(s_t, axis=0, keepdims=True)
        return mc if m is None else jnp.maximum(m, mc)

    def pv_chunk(u, c, m, acc):
        _, h = units[u]
        keys = slice(c * ck, (c + 1) * ck)
        s_t = s_refs[u % 2][pl.ds(pl.multiple_of(c * ck + dyn0, ck), ck), :]
        p = jnp.exp2((s_t - m).astype(BF16))
        vt_ext = jnp.concatenate([vt_ref[0, h * MLA_V:(h + 1) * MLA_V, keys], ones], axis=0)
        d = _dot(vt_ext, p)
        return d if acc is None else acc + d

    m = [None] * (len(units) + 1)
    for c in range(n_chunks):
        m[0] = qk_chunk(0, c, m[0])
    for u, (qt, h) in enumerate(units):
        acc = None
        for c in range(n_chunks):
            if u + 1 < len(units):
                m[u + 1] = qk_chunk(u + 1, c, m[u + 1])
            acc = pv_chunk(u, c, m[u], acc)
        o_ref[0, h * MLA_V:(h + 1) * MLA_V, qt * qw:(qt + 1) * qw] = (
            acc[:MLA_V] / acc[MLA_V:MLA_V + 1]).astype(BF16)


def _mla_attn(q, k, vt, tq):
    batch, seq, hp = q.shape
    hv = MLA_HEADS * MLA_V
    return pl.pallas_call(
        _mla_attn_kernel,
        grid=(batch, seq // tq),
        in_specs=[
            pl.BlockSpec((1, tq, hp), lambda b, i: (b, i, 0)),
            pl.BlockSpec((1, seq, hp), lambda b, i: (b, 0, 0)),
            pl.BlockSpec((1, hv, seq), lambda b, i: (b, 0, 0)),
        ],
        out_specs=pl.BlockSpec((1, hv, tq), lambda b, i: (b, 0, i)),
        out_shape=jax.ShapeDtypeStruct((batch, hv, seq), BF16),
        scratch_shapes=[pltpu.VMEM((seq, min(MLA_Q_SUB, tq)), F32)] * 2,
        compiler_params=_params(2),
        name="mla_attn",
    )(q, k, vt)


def _split2(x):
    hi = x.astype(BF16)
    mid = (x - hi.astype(F32)).astype(BF16)
    return hi, mid


def _gla_kernel(q_ref, k_ref, v_ref, r_ref, la_ref, ng_ref, o_ref,
                state_ref, ofwd_ref, oblk_ref, dstate_ref, decay_ref, states_ref,
                *, n_blocks, tb):
    phase = pl.program_id(1)
    j = pl.program_id(2)
    is_fwd = phase == 0
    n_chunks = tb // GLA_CHUNK
    C = GLA_CHUNK

    @pl.when(j == 0)
    def _():
        state_ref[...] = jnp.zeros_like(state_ref)

    G = min(GLA_GROUP, tb)
    cpg = G // C
    n_groups = tb // G
    H = GLA_HEADS

    sign = jnp.where(is_fwd, 1, -1)
    far = 4 * G

    def rel_matrix(rows):
        t = lax.broadcasted_iota(jnp.int32, (rows, G), 0) & (G - 1)
        s = lax.broadcasted_iota(jnp.int32, (rows, G), 1)
        same_chunk = ((t ^ s) & -C) == 0
        return jnp.where(same_chunk, (s - t) * sign, far)

    cum_mat = jnp.where(rel_matrix(G) <= 0, 1.0, 0.0).astype(BF16)
    att_keep = rel_matrix(H * G) <= jnp.where(is_fwd, 0, -1)
    chunk_rows = (lax.broadcasted_iota(jnp.int32, (cpg * GLA_DK, G), 0) // GLA_DK
                  == lax.broadcasted_iota(jnp.int32, (cpg * GLA_DK, G), 1) // C)
    lane_head_g = lax.broadcasted_iota(jnp.int32, (G, GLA_QK), 1) // GLA_DK
    lane_head_c = lax.broadcasted_iota(jnp.int32, (C, GLA_QK), 1) // GLA_DK

    groups = range(n_groups)
    rows = [slice(g * G, (g + 1) * G) for g in groups]
    las = [la_ref[0, rows[g], :] for g in groups]
    splits = [_split2(la) for la in las]
    bs = [_dot(cum_mat, hi) + _dot(cum_mat, mid) for hi, mid in splits]
    tots = [[jnp.sum(la[c * C:(c + 1) * C], axis=0, keepdims=True) for c in range(cpg)]
            for la in las]
    b_tots = [jnp.concatenate([jnp.broadcast_to(t, (C, GLA_QK)) for t in tots[g]], axis=0)
              for g in groups]
    q_ins = [(q_ref[rows[g], :] * jnp.exp(bs[g])).astype(BF16) for g in groups]
    k_ins = [(k_ref[rows[g], :] * jnp.exp(-bs[g])).astype(BF16) for g in groups]
    k_st_ts = [jnp.transpose(k_ref[rows[g], :] * jnp.exp(b_tots[g] - bs[g])) for g in groups]
    q_stacks = [jnp.concatenate(
        [jnp.where(lane_head_g == h, q_ins[g], jnp.zeros_like(q_ins[g])) for h in range(H)],
        axis=0) for g in groups]
    atts = [jnp.where(att_keep, _dot_nt(q_stacks[g], k_ins[g]), 0.0).astype(BF16)
            for g in groups]
    for g in groups:
        for h in range(H):
            k_t = k_st_ts[g][h * GLA_DK:(h + 1) * GLA_DK, :]
            lhs = jnp.where(chunk_rows, jnp.concatenate([k_t] * cpg, axis=0), 0.0).astype(BF16)
            d_h = _dot(lhs, v_ref[rows[g], h * GLA_DV:(h + 1) * GLA_DV])
            for c in range(cpg):
                dstate_ref[g * cpg + c, h * GLA_DK:(h + 1) * GLA_DK, :] = (
                    d_h[c * GLA_DK:(c + 1) * GLA_DK])
        for c in range(cpg):
            decay_ref[g * cpg + c] = jnp.exp(
                jnp.transpose(jnp.broadcast_to(tots[g][c], (GLA_DV, GLA_QK))))
    o_intras = [jnp.concatenate(
        [_dot(atts[g][h * G:(h + 1) * G], v_ref[rows[g], h * GLA_DV:(h + 1) * GLA_DV])
         for h in range(H)], axis=1) for g in groups]
    q_in_chunks = [q_ins[g][c * C:(c + 1) * C] for g in groups for c in range(cpg)]
    o_intra_chunks = [o_intras[g][c * C:(c + 1) * C] for g in groups for c in range(cpg)]

    state = state_ref[...]
    for i in range(n_chunks):
        ch = jnp.where(is_fwd, i, n_chunks - 1 - i)
        states_ref[ch] = state
        state = decay_ref[ch] * state + dstate_ref[ch]
    state_ref[...] = state

    for ch in range(n_chunks):
        q_stack = jnp.concatenate(
            [jnp.where(lane_head_c == h, q_in_chunks[ch], jnp.zeros_like(q_in_chunks[ch]))
             for h in range(H)], axis=0)
        inter = _dot(q_stack, states_ref[ch].astype(BF16))
        oblk_ref[ch * C:(ch + 1) * C, :] = o_intra_chunks[ch] + jnp.concatenate(
            [inter[h * C:(h + 1) * C] for h in range(H)], axis=1)

    blk = jnp.where(is_fwd, j, n_blocks - 1 - j)
    row0 = pl.multiple_of(blk * tb, tb)

    @pl.when(is_fwd)
    def _():
        ofwd_ref[pl.ds(row0, tb), :] = oblk_ref[...]

    @pl.when(jnp.logical_not(is_fwd))
    def _():
        ng = ng_ref[...]
        for r in range(0, tb, C):
            o = ofwd_ref[pl.ds(pl.multiple_of(row0 + r, C), C), :] + oblk_ref[r:r + C, :]
            normed = jnp.concatenate(
                [_rms_norm(o[:, h * GLA_DV:(h + 1) * GLA_DV], ng) for h in range(H)], axis=1)
            o_ref[r:r + C, :] = (normed * _silu(r_ref[r:r + C, :].astype(F32))).astype(BF16)


def _gla(gq, gk, gv, gr, la, norm_g, batch, seq, tb):
    nb = seq // tb

    def blk(ph, j):
        return jnp.where(ph == 0, j, nb - 1 - j)

    def row(width):
        return pl.BlockSpec((tb, width), lambda b, ph, j: (b * nb + blk(ph, j), 0))

    out_spec = pl.BlockSpec((tb, GLA_VW), lambda b, ph, j: (b * nb + nb - 1 - ph * j, 0))
    kern = functools.partial(_gla_kernel, n_blocks=nb, tb=tb)
    return pl.pallas_call(
        kern,
        grid=(batch, 2, nb),
        in_specs=[row(GLA_QK), row(GLA_QK), row(GLA_VW), row(GLA_VW),
                  pl.BlockSpec((1, tb, GLA_QK), lambda b, ph, j: (ph, b * nb + blk(ph, j), 0)),
                  _resident(norm_g.shape)],
        out_specs=out_spec,
        out_shape=jax.ShapeDtypeStruct((batch * seq, GLA_VW), BF16),
        scratch_shapes=[pltpu.VMEM((GLA_QK, GLA_DV), F32),
                        pltpu.VMEM((seq, GLA_VW), F32),
                        pltpu.VMEM((tb, GLA_VW), F32),
                        pltpu.VMEM((tb // GLA_CHUNK, GLA_QK, GLA_DV), F32),
                        pltpu.VMEM((tb // GLA_CHUNK, GLA_QK, GLA_DV), F32),
                        pltpu.VMEM((tb // GLA_CHUNK, GLA_QK, GLA_DV), F32)],
        compiler_params=_params(3),
        name="gla",
    )(gq, gk, gv, gr, la, norm_g)


def _outproj_ln_kernel(x_ref, ot_ref, og_ref, wa_ref, wb_ref, g_ref, b_ref, o_ref):
    y = _dot_tn(ot_ref[0], wa_ref[...]) + _dot(og_ref[...], wb_ref[...])
    _layer_norm_rows(o_ref, lambda r: DN_ALPHA * x_ref[r, :] + y[r, :], g_ref[...], b_ref[...])


def _outproj_ln(x2, ot, og, wa, wb, g, b, seq, tm):
    n = x2.shape[0]
    nt = seq // tm
    row = lambda width: pl.BlockSpec((tm, width), lambda i: (i, 0))
    return pl.pallas_call(
        _outproj_ln_kernel,
        grid=(n // tm,),
        in_specs=[row(D_MODEL),
                  pl.BlockSpec((1, ot.shape[1], tm), lambda i: (i // nt, 0, i % nt)),
                  row(GLA_VW),
                  _resident(wa.shape), _resident(wb.shape), _resident(g.shape), _resident(b.shape)],
        out_specs=row(D_MODEL),
        out_shape=jax.ShapeDtypeStruct((n, D_MODEL), F32),
        compiler_params=_params(1),
        name="outproj_ln",
    )(x2, ot, og, wa, wb, g, b)


def _sgu_ln_kernel(x_ref, win_ref, lng_ref, lnb_ref, ws_ref, bs_ref, wout_ref, g_ref, b_ref,
                   o_ref, v_ref, vn_ref, p_ref, z_ref):
    i = pl.program_id(0)

    @pl.when(i == 0)
    def _():
        z_ref[1] = jnp.zeros(z_ref.shape[1:], F32)

    prev = (i + 1) % 2
    x = x_ref[...]
    xb = x.astype(BF16)
    tm = x.shape[0]
    n_col = SGU_HALF // SGU_COLS
    ln_rows = tm // n_col
    for jc in range(n_col):
        lo = SGU_HALF + jc * SGU_COLS
        v_ref[:, jc * SGU_COLS:(jc + 1) * SGU_COLS] = _gelu(_dot(xb, win_ref[:, lo:lo + SGU_COLS]))
    def u_chunk(jc):
        hu = _dot(xb, win_ref[:, jc * SGU_COLS:(jc + 1) * SGU_COLS])
        r0 = jc * ln_rows
        _layer_norm_rows(o_ref.at[r0:r0 + ln_rows, :],
                         lambda r: z_ref[prev, r0 + r.start:r0 + r.stop, :],
                         g_ref[...], b_ref[...])
        return _gelu(hu + _zero_after_store(o_ref, i))

    u_next = u_chunk(0)
    lng = lng_ref[...]
    lnb = lnb_ref[...]
    for r in range(0, tm, SGU_LN_ROWS):
        vn_ref[r:r + SGU_LN_ROWS, :] = _layer_norm(v_ref[r:r + SGU_LN_ROWS, :], lng, lnb).astype(BF16)

    for jc in range(n_col):
        u = u_next
        if jc + 1 < n_col:
            u_next = u_chunk(jc + 1)
        for gg in range(SGU_COLS // SGU_GROUP_DIM):
            grp = jc * (SGU_COLS // SGU_GROUP_DIM) + gg
            c0 = grp * SGU_GROUP_DIM
            bias = jnp.concatenate([bs_ref[grp]] * (SGU_GROUP_DIM // LANE), axis=1)
            for n in range(tm // SGU_CHUNK):
                t0 = n * SGU_CHUNK
                mixed = _dot(ws_ref[grp], vn_ref[t0:t0 + SGU_CHUNK, c0:c0 + SGU_GROUP_DIM]) + bias
                ug = u[t0:t0 + SGU_CHUNK, gg * SGU_GROUP_DIM:(gg + 1) * SGU_GROUP_DIM]
                p_ref[t0:t0 + SGU_CHUNK, c0:c0 + SGU_GROUP_DIM] = (ug * mixed).astype(BF16)
    y = _dot(p_ref[...], wout_ref[...])
    z_ref[i % 2] = DN_ALPHA * x + y


def _sgu_ln(x2, win, lng, lnb, ws, bs, wout, g, b, tm):
    n = x2.shape[0]
    nt = n // tm
    row = pl.BlockSpec((tm, D_MODEL), lambda i: (jnp.minimum(i, nt - 1), 0))
    return pl.pallas_call(
        _sgu_ln_kernel,
        grid=(nt + 1,),
        in_specs=[row, _resident(win.shape), _resident(lng.shape), _resident(lnb.shape),
                  _resident(ws.shape), _resident(bs.shape), _resident(wout.shape),
                  _resident(g.shape), _resident(b.shape)],
        out_specs=pl.BlockSpec((tm, D_MODEL), lambda i: (jnp.maximum(i - 1, 0), 0)),
        out_shape=jax.ShapeDtypeStruct((n, D_MODEL), F32),
        scratch_shapes=[pltpu.VMEM((tm, SGU_HALF), F32),
                        pltpu.VMEM((tm, SGU_HALF), BF16),
                        pltpu.VMEM((tm, SGU_HALF), BF16),
                        pltpu.VMEM((2, tm, D_MODEL), F32)],
        compiler_params=_params(1),
        name="sgu_ln",
    )(x2, win, lng, lnb, ws, bs, wout, g, b)


def _row(v):
    return v.reshape(1, -1).astype(F32)


def _pack_ffn(w_gu, w_down):
    return w_gu.astype(BF16), w_down.astype(BF16)


def _rope_tab(seq):
    inv_freq = 1.0 / (ROPE_BASE ** (jnp.arange(0, MLA_ROPE, 2, dtype=F32) / MLA_ROPE))
    ang = jnp.arange(seq, dtype=F32)[:, None] * inv_freq[None, :]
    cos, sin = jnp.cos(ang), jnp.sin(ang)
    pad_lo = jnp.zeros((seq, MLA_NOPE), F32)
    pad_hi = jnp.zeros((seq, MLA_HEAD_PAD - MLA_NOPE - MLA_ROPE), F32)
    k_dir = jnp.concatenate([pad_lo, cos, cos, pad_hi], axis=1)
    k_swp = jnp.concatenate([pad_lo, -sin, sin, pad_hi], axis=1)
    q_scale = (MLA_NOPE + MLA_ROPE) ** -0.5 * math.log2(math.e)
    q_dir = jnp.concatenate([jnp.ones((seq, MLA_NOPE), F32), cos, cos, pad_hi], axis=1) * q_scale
    q_swp = k_swp * q_scale
    return jnp.stack([q_dir, q_swp, k_dir, k_swp])


def _pack_even(w_in, q_norm, w_uq, kv_norm, w_ukv, wg_f, bg_f, wg_b, bg_b, seqs):
    o = 0
    parts = {}
    for name, width in (("cq", MLA_Q_RANK), ("ckv", MLA_KV_RANK), ("kr", MLA_ROPE),
                        ("q", GLA_QK), ("k", GLA_QK), ("v", GLA_VW), ("r", GLA_VW),
                        ("zf", GLA_GATE_RANK), ("zb", GLA_GATE_RANK)):
        parts[name] = w_in[:, o:o + width]
        o += width
    zcol = lambda n: jnp.zeros((D_MODEL, n), F32)
    m1 = jnp.concatenate([parts["zf"], parts["zb"], zcol(MLA_NOPE - 2 * GLA_GATE_RANK),
                          parts["kr"], zcol(MLA_HEAD_PAD - MLA_NOPE - MLA_ROPE)], axis=1)
    win = jnp.concatenate([parts["cq"], parts["ckv"], parts["q"] * (GLA_DK ** -0.5), parts["k"],
                           parts["v"], parts["r"], m1], axis=1).astype(BF16)

    qk = MLA_NOPE + MLA_ROPE
    uq = w_uq.reshape(MLA_Q_RANK, MLA_HEADS, qk)
    pad = MLA_HEAD_PAD - qk
    wqa = jnp.pad(uq, ((0, 0), (0, 0), (0, pad))).reshape(MLA_Q_RANK, -1).astype(BF16)

    ukv = w_ukv.reshape(MLA_KV_RANK, MLA_HEADS, MLA_NOPE + MLA_V)
    wka = jnp.pad(ukv[:, :, :MLA_NOPE], ((0, 0), (0, 0), (0, MLA_HEAD_PAD - MLA_NOPE)))
    wka = wka.reshape(MLA_KV_RANK, -1).astype(BF16)
    wvt = ukv[:, :, MLA_NOPE:].reshape(MLA_KV_RANK, -1).T.astype(BF16)

    gpad = lambda w, at: jnp.pad(w, ((at, LANE - at - GLA_GATE_RANK), (0, 0)))
    wg = jnp.stack([gpad(wg_f, 0), gpad(wg_b, GLA_GATE_RANK)]).astype(BF16)
    bg = jnp.stack([_row(bg_f), _row(bg_b)])
    return dict(win=win, q_norm=_row(q_norm), wqa=wqa, kv_norm=_row(kv_norm),
                wka=wka, wvt=wvt, wg=wg, bg=bg,
                rope_tab={s: _rope_tab(s) for s in seqs})


def _tile(n, pref):
    return pref if n % pref == 0 else n


def kernel(x_prompt, x_sample, l0_ffa_w_gu, l0_ffa_w_down, l0_ln1_g, l0_ln1_b, l0_w_in, l0_mla_q_norm, l0_mla_w_uq, l0_mla_kv_norm, l0_mla_w_ukv, l0_gla_w_gate_f, l0_gla_b_gate_f, l0_gla_w_gate_b, l0_gla_b_gate_b, l0_gla_norm, l0_w_out, l0_ln2_g, l0_ln2_b, l0_ffb_w_gu, l0_ffb_w_down, l0_ln3_g, l0_ln3_b, l1_ffa_w_gu, l1_ffa_w_down, l1_ln1_g, l1_ln1_b, l1_sgu_w_in, l1_sgu_ln_g, l1_sgu_ln_b, l1_sgu_w_s, l1_sgu_b_s, l1_sgu_w_out, l1_ln2_g, l1_ln2_b, l1_ffb_w_gu, l1_ffb_w_down, l1_ln3_g, l1_ln3_b):
    seqs = sorted({x_prompt.shape[1], x_sample.shape[1]})
    ffn = [
        _pack_ffn(l0_ffa_w_gu, l0_ffa_w_down) + (_row(l0_ln1_g), _row(l0_ln1_b)),
        _pack_ffn(l0_ffb_w_gu, l0_ffb_w_down) + (_row(l0_ln3_g), _row(l0_ln3_b)),
        _pack_ffn(l1_ffa_w_gu, l1_ffa_w_down) + (_row(l1_ln1_g), _row(l1_ln1_b)),
        _pack_ffn(l1_ffb_w_gu, l1_ffb_w_down) + (_row(l1_ln3_g), _row(l1_ln3_b)),
    ]
    even = _pack_even(l0_w_in, l0_mla_q_norm, l0_mla_w_uq, l0_mla_kv_norm, l0_mla_w_ukv,
                      l0_gla_w_gate_f, l0_gla_b_gate_f, l0_gla_w_gate_b, l0_gla_b_gate_b, seqs)
    hv = MLA_HEADS * MLA_V
    w_out_a = l0_w_out[:hv].astype(BF16)
    w_out_b = l0_w_out[hv:].astype(BF16)
    gla_norm = _row(l0_gla_norm)
    sgu_bs = jnp.broadcast_to(l1_sgu_b_s[:, :, None], (SGU_GROUPS, SGU_CHUNK, LANE)).astype(F32)
    sgu = (l1_sgu_w_in.astype(BF16), _row(l1_sgu_ln_g), _row(l1_sgu_ln_b),
           l1_sgu_w_s.astype(BF16), sgu_bs, l1_sgu_w_out.astype(BF16),
           _row(l1_ln2_g), _row(l1_ln2_b))

    def trunk(x):
        batch, seq, _ = x.shape
        tm_wide = _tile(seq, 1024)
        tm_half = _tile(seq, 512)
        x2 = x.reshape(batch * seq, D_MODEL)
        x2 = _ffn_ln(x2, *ffn[0], tm_wide)
        q, k, vt, gq, gk, gv, gr, la = _proj_ab(x2, batch, seq, even, tm_wide)
        hp = MLA_HEADS * MLA_HEAD_PAD
        ot = _mla_attn(q.reshape(batch, seq, hp), k.reshape(batch, seq, hp), vt, tm_half)
        og = _gla(gq, gk, gv, gr, la, gla_norm, batch, seq, tm_wide)
        x2 = _outproj_ln(x2, ot, og, w_out_a, w_out_b, _row(l0_ln2_g), _row(l0_ln2_b), seq, tm_wide)
        x2 = _ffn_ln(x2, *ffn[1], tm_wide)
        x2 = _ffn_ln(x2, *ffn[2], tm_wide)
        x2 = _sgu_ln(x2, *sgu, tm_half)
        x2 = _ffn_ln(x2, *ffn[3], tm_wide)
        return x2.reshape(batch, seq, D_MODEL)

    return (trunk(x_prompt), trunk(x_sample))
```

```python
import functools
import math

import jax
import jax.numpy as jnp
from jax import lax
from jax.experimental import pallas as pl
from jax.experimental.pallas import tpu as pltpu

F32 = jnp.float32
BF16 = jnp.bfloat16

D_MODEL = 1024
DEPTH = 2
DN_ALPHA = (2 * DEPTH) ** 0.25
LN_EPS = 1e-5
RMS_EPS = 1e-6

MLA_HEADS = 8
MLA_Q_RANK = 768
MLA_KV_RANK = 256
MLA_NOPE = 64
MLA_ROPE = 32
MLA_V = 64
MLA_HEAD_PAD = 128
ROPE_BASE = 10000.0
MLA_KEY_CHUNK = 1024
MLA_Q_SUB = 256
MLA_ONES_ROWS = 16

GLA_HEADS = 4
GLA_DK = 64
GLA_DV = 128
GLA_GATE_RANK = 16
GLA_TAU = 16.0
GLA_CHUNK = 64
GLA_GROUP = 256
GLA_QK = GLA_HEADS * GLA_DK
GLA_VW = GLA_HEADS * GLA_DV

SGU_CHUNK = 128
SGU_HALF = 3072
SGU_GROUPS = 8
SGU_GROUP_DIM = SGU_HALF // SGU_GROUPS
SGU_COLS = 2 * SGU_GROUP_DIM
SGU_LN_ROWS = 16

FFN_HIDDEN = 2816
FFN_CHUNK = 256

LANE = 128
LN_ROWS = 32
VMEM_LIMIT_BYTES = 56 * 1024 * 1024

_P_CQ = 0
_P_CKV = _P_CQ + MLA_Q_RANK
_P_GQ = _P_CKV + MLA_KV_RANK
_P_GK = _P_GQ + GLA_QK
_P_GV = _P_GK + GLA_QK
_P_GR = _P_GV + GLA_VW
_P_M1 = _P_GR + GLA_VW
_P_END = _P_M1 + LANE


def _resident(shape):
    zeros = (0,) * len(shape)
    return pl.BlockSpec(shape, lambda *_: zeros, pipeline_mode=pl.Buffered(1))


def _params(n_axes):
    return pltpu.CompilerParams(
        dimension_semantics=("arbitrary",) * n_axes,
        vmem_limit_bytes=VMEM_LIMIT_BYTES,
    )


def _layer_norm(y, g, b):
    mu = jnp.mean(y, axis=-1, keepdims=True)
    yc = y - mu
    var = jnp.mean(yc * yc, axis=-1, keepdims=True)
    return yc * lax.rsqrt(var + LN_EPS) * g + b


def _layer_norm_rows(o_ref, load, g, b):
    for r in range(0, o_ref.shape[0], LN_ROWS):
        rows = slice(r, r + LN_ROWS)
        o_ref[rows, :] = _layer_norm(load(rows), g, b)


def _rms_norm(y, g):
    ms = jnp.mean(y * y, axis=-1, keepdims=True)
    return y * lax.rsqrt(ms + RMS_EPS) * g


def _gelu(x):
    return 0.5 * x * (1.0 + lax.erf(x * math.sqrt(0.5)))


def _silu(x):
    return x * jax.nn.sigmoid(x)


def _log_sigmoid(z):
    return jnp.minimum(z, 0.0) - jnp.log1p(jnp.exp(-jnp.abs(z)))


def _zero_after_store(ref, step):
    row0 = pl.multiple_of(jnp.minimum(step, 0), 8)
    words = pltpu.bitcast(ref[pl.ds(row0, 8), :LANE], jnp.uint32)
    zero = pltpu.bitcast((words >> 16) >> 16, F32)
    return zero[:1, :1]


def _dot(a, b):
    return jnp.dot(a, b, preferred_element_type=F32)


def _dot_nt(a, b):
    return lax.dot_general(a, b, (((1,), (1,)), ((), ())), preferred_element_type=F32)


def _dot_tn(a, b):
    return lax.dot_general(a, b, (((0,), (0,)), ((), ())), preferred_element_type=F32)


def _ffn_ln_kernel(x_ref, wgu_ref, wd_ref, g_ref, b_ref, o_ref, a_ref, z_ref, *, n_tiles):
    i = pl.program_id(0)

    @pl.when(i == 0)
    def _():
        z_ref[1] = jnp.zeros(z_ref.shape[1:], F32)

    prev = (i + 1) % 2
    tm = o_ref.shape[0]
    n_chunks = FFN_HIDDEN // FFN_CHUNK
    ln_chunks = n_chunks - 3
    ln_rows = tm // ln_chunks if tm % ln_chunks == 0 else tm
    last = n_tiles

    @pl.when(i < last)
    def _():
        xb = x_ref[...].astype(BF16)
        for c in range(n_chunks):
            lo = c * FFN_CHUNK
            gate = _dot(xb, wgu_ref[:, lo:lo + FFN_CHUNK])
            up = _dot(xb, wgu_ref[:, FFN_HIDDEN + lo:FFN_HIDDEN + lo + FFN_CHUNK])
            if c * ln_rows < tm:
                r0 = c * ln_rows
                _layer_norm_rows(o_ref.at[r0:r0 + ln_rows, :],
                                 lambda r: z_ref[prev, r0 + r.start:r0 + r.stop, :],
                                 g_ref[...], b_ref[...])
                up = up + _zero_after_store(o_ref, i)
            a_ref[:, lo:lo + FFN_CHUNK] = (_silu(gate) * up).astype(BF16)
        y = _dot(a_ref[...], wd_ref[...])
        z_ref[i % 2] = DN_ALPHA * x_ref[...] + 0.5 * y

    @pl.when(i == last)
    def _():
        _layer_norm_rows(o_ref, lambda r: z_ref[prev, r, :], g_ref[...], b_ref[...])


def _ffn_ln(x2, wgu, wd, g, b, tm):
    n = x2.shape[0]
    nt = n // tm
    return pl.pallas_call(
        functools.partial(_ffn_ln_kernel, n_tiles=nt),
        grid=(nt + 1,),
        in_specs=[pl.BlockSpec((tm, D_MODEL), lambda i: (jnp.minimum(i, nt - 1), 0)),
                  _resident(wgu.shape), _resident(wd.shape),
                  _resident(g.shape), _resident(b.shape)],
        out_specs=pl.BlockSpec((tm, D_MODEL), lambda i: (jnp.maximum(i - 1, 0), 0)),
        out_shape=jax.ShapeDtypeStruct((n, D_MODEL), F32),
        scratch_shapes=[pltpu.VMEM((tm, FFN_HIDDEN), BF16),
                        pltpu.VMEM((2, tm, D_MODEL), F32)],
        compiler_params=_params(1),
        name="ffn_ln",
    )(x2, wgu, wd, g, b)


def _swap_rope_halves(x):
    n = x.shape[1]
    half = MLA_ROPE // 2
    lane = lax.broadcasted_iota(jnp.int32, x.shape, 1) & (MLA_HEAD_PAD - 1)
    from_above = pltpu.roll(x, n - half, axis=1)
    from_below = pltpu.roll(x, half, axis=1)
    return jnp.where(lane < MLA_NOPE + half, from_above, from_below)


def _proj_ab_kernel(x_ref, win_ref, tab_ref, qn_ref, wqa_ref, kvn_ref, wka_ref,
                    wvt_ref, wg_ref, bg_ref,
                    q_ref, k_ref, vt_ref, gq_ref, gk_ref, gv_ref, gr_ref, la_ref):
    xb = x_ref[...].astype(BF16)
    p = _dot(xb, win_ref[...])
    gq_ref[...] = p[:, _P_GQ:_P_GK]
    gk_ref[...] = p[:, _P_GK:_P_GV]
    gv_ref[...] = p[:, _P_GV:_P_GR].astype(BF16)
    gr_ref[...] = p[:, _P_GR:_P_M1].astype(BF16)
    m1 = p[:, _P_M1:_P_END]

    qn = _rms_norm(p[:, _P_CQ:_P_CKV], qn_ref[...]).astype(BF16)
    cq = jnp.concatenate([tab_ref[0]] * MLA_HEADS, axis=1)
    sq = jnp.concatenate([tab_ref[1]] * MLA_HEADS, axis=1)
    qa = _dot(qn, wqa_ref[...])
    q_ref[...] = (qa * cq + _swap_rope_halves(qa) * sq).astype(BF16)

    kvn = _rms_norm(p[:, _P_CKV:_P_GQ], kvn_ref[...]).astype(BF16)
    k_rope = m1 * tab_ref[2] + _swap_rope_halves(m1) * tab_ref[3]
    k_ref[...] = (_dot(kvn, wka_ref[...])
                  + jnp.concatenate([k_rope] * MLA_HEADS, axis=1)).astype(BF16)
    vt_ref[0] = _dot_nt(wvt_ref[...], kvn).astype(BF16)

    m1b = m1.astype(BF16)
    for d in range(2):
        z = _dot(m1b, wg_ref[d]) + bg_ref[d]
        la_ref[d] = _log_sigmoid(z) * (1.0 / GLA_TAU)


def _proj_ab(x2, batch, seq, w, tm):
    n = x2.shape[0]
    nt = seq // tm
    row = lambda width: pl.BlockSpec((tm, width), lambda i: (i, 0))
    in_specs = [
        row(D_MODEL),
        _resident(w["win"].shape),
        pl.BlockSpec((4, tm, LANE), lambda i: (0, i % nt, 0)),
        _resident(w["q_norm"].shape), _resident(w["wqa"].shape),
        _resident(w["kv_norm"].shape), _resident(w["wka"].shape), _resident(w["wvt"].shape),
        _resident(w["wg"].shape), _resident(w["bg"].shape),
    ]
    hp = MLA_HEADS * MLA_HEAD_PAD
    hv = MLA_HEADS * MLA_V
    out_specs = [
        row(hp), row(hp),
        pl.BlockSpec((1, hv, tm), lambda i: (i // nt, 0, i % nt)),
        row(GLA_QK), row(GLA_QK), row(GLA_VW), row(GLA_VW),
        pl.BlockSpec((2, tm, GLA_QK), lambda i: (0, i, 0)),
    ]
    out_shape = [
        jax.ShapeDtypeStruct((n, hp), BF16), jax.ShapeDtypeStruct((n, hp), BF16),
        jax.ShapeDtypeStruct((batch, hv, seq), BF16),
        jax.ShapeDtypeStruct((n, GLA_QK), F32), jax.ShapeDtypeStruct((n, GLA_QK), F32),
        jax.ShapeDtypeStruct((n, GLA_VW), BF16), jax.ShapeDtypeStruct((n, GLA_VW), BF16),
        jax.ShapeDtypeStruct((2, n, GLA_QK), F32),
    ]
    return pl.pallas_call(
        _proj_ab_kernel,
        grid=(n // tm,),
        in_specs=in_specs,
        out_specs=out_specs,
        out_shape=out_shape,
        compiler_params=_params(1),
        name="proj_ab",
    )(x2, w["win"], w["rope_tab"][seq], w["q_norm"], w["wqa"], w["kv_norm"],
      w["wka"], w["wvt"], w["wg"], w["bg"])


def _mla_attn_kernel(q_ref, k_ref, vt_ref, o_ref, sa_ref, sb_ref):
    seq = k_ref.shape[1]
    ck = min(MLA_KEY_CHUNK, seq)
    n_chunks = seq // ck
    ones = jnp.ones((MLA_ONES_ROWS, ck), BF16)
    s_refs = (sa_ref, sb_ref)
    dyn0 = jnp.minimum(pl.program_id(0), 0)

    tq = q_ref.shape[1]
    qw = min(MLA_Q_SUB, tq)
    units = [(qt, h) for qt in range(tq // qw) for h in range(MLA_HEADS)]

    def qk_chunk(u, c, m):
        qt, h = units[u]
        lanes = slice(h * MLA_HEAD_PAD, (h + 1) * MLA_HEAD_PAD)
        keys = slice(c * ck, (c + 1) * ck)
        s_t = _dot_nt(k_ref[0, keys, lanes], q_ref[0, qt * qw:(qt + 1) * qw, lanes])
        s_refs[u % 2][pl.ds(pl.multiple_of(c * ck + dyn0, ck), ck), :] = s_t
        mc = jnp.max(s_t, axis=0, keepdims=True)
        return mc if m is None else jnp.maximum(m, mc)

    def pv_chunk(u, c, m, acc):
        _, h = units[u]
        keys = slice(c * ck, (c + 1) * ck)
        s_t = s_refs[u % 2][pl.ds(pl.multiple_of(c * ck + dyn0, ck), ck), :]
        p = jnp.exp2((s_t - m).astype(BF16))
        vt_ext = jnp.concatenate([vt_ref[0, h * MLA_V:(h + 1) * MLA_V, keys], ones], axis=0)
        d = _dot(vt_ext, p)
        return d if acc is None else acc + d

    m = [None] * (len(units) + 1)
    for c in range(n_chunks):
        m[0] = qk_chunk(0, c, m[0])
    for u, (qt, h) in enumerate(units):
        acc = None
        for c in range(n_chunks):
            if u + 1 < len(units):
                m[u + 1] = qk_chunk(u + 1, c, m[u + 1])
            acc = pv_chunk(u, c, m[u], acc)
        o_ref[0, h * MLA_V:(h + 1) * MLA_V, qt * qw:(qt + 1) * qw] = (
            acc[:MLA_V] / acc[MLA_V:MLA_V + 1]).astype(BF16)


def _mla_attn(q, k, vt, tq):
    batch, seq, hp = q.shape
    hv = MLA_HEADS * MLA_V
    return pl.pallas_call(
        _mla_attn_kernel,
        grid=(batch, seq // tq),
        in_specs=[
            pl.BlockSpec((1, tq, hp), lambda b, i: (b, i, 0)),
            pl.BlockSpec((1, seq, hp), lambda b, i: (b, 0, 0)),
            pl.BlockSpec((1, hv, seq), lambda b, i: (b, 0, 0)),
        ],
        out_specs=pl.BlockSpec((1, hv, tq), lambda b, i: (b, 0, i)),
        out_shape=jax.ShapeDtypeStruct((batch, hv, seq), BF16),
        scratch_shapes=[pltpu.VMEM((seq, min(MLA_Q_SUB, tq)), F32)] * 2,
        compiler_params=_params(2),
        name="mla_attn",
    )(q, k, vt)


def _split2(x):
    hi = x.astype(BF16)
    mid = (x - hi.astype(F32)).astype(BF16)
    return hi, mid


def _gla_kernel(q_ref, k_ref, v_ref, r_ref, la_ref, ng_ref, o_ref,
                state_ref, ofwd_ref, oblk_ref, dstate_ref, decay_ref, states_ref,
                *, n_blocks, tb):
    phase = pl.program_id(1)
    j = pl.program_id(2)
    is_fwd = phase == 0
    n_chunks = tb // GLA_CHUNK
    C = GLA_CHUNK

    @pl.when(j == 0)
    def _():
        state_ref[...] = jnp.zeros_like(state_ref)

    G = min(GLA_GROUP, tb)
    cpg = G // C
    n_groups = tb // G
    H = GLA_HEADS

    sign = jnp.where(is_fwd, 1, -1)
    far = 4 * G

    def rel_matrix(rows):
        t = lax.broadcasted_iota(jnp.int32, (rows, G), 0) & (G - 1)
        s = lax.broadcasted_iota(jnp.int32, (rows, G), 1)
        same_chunk = ((t ^ s) & -C) == 0
        return jnp.where(same_chunk, (s - t) * sign, far)

    cum_mat = jnp.where(rel_matrix(G) <= 0, 1.0, 0.0).astype(BF16)
    att_keep = rel_matrix(H * G) <= jnp.where(is_fwd, 0, -1)
    chunk_rows = (lax.broadcasted_iota(jnp.int32, (cpg * GLA_DK, G), 0) // GLA_DK
                  == lax.broadcasted_iota(jnp.int32, (cpg * GLA_DK, G), 1) // C)
    lane_head_g = lax.broadcasted_iota(jnp.int32, (G, GLA_QK), 1) // GLA_DK
    lane_head_c = lax.broadcasted_iota(jnp.int32, (C, GLA_QK), 1) // GLA_DK

    groups = range(n_groups)
    rows = [slice(g * G, (g + 1) * G) for g in groups]
    las = [la_ref[0, rows[g], :] for g in groups]
    splits = [_split2(la) for la in las]
    bs = [_dot(cum_mat, hi) + _dot(cum_mat, mid) for hi, mid in splits]
    tots = [[jnp.sum(la[c * C:(c + 1) * C], axis=0, keepdims=True) for c in range(cpg)]
            for la in las]
    b_tots = [jnp.concatenate([jnp.broadcast_to(t, (C, GLA_QK)) for t in tots[g]], axis=0)
              for g in groups]
    q_ins = [(q_ref[rows[g], :] * jnp.exp(bs[g])).astype(BF16) for g in groups]
    k_ins = [(k_ref[rows[g], :] * jnp.exp(-bs[g])).astype(BF16) for g in groups]
    k_st_ts = [jnp.transpose(k_ref[rows[g], :] * jnp.exp(b_tots[g] - bs[g])) for g in groups]
    q_stacks = [jnp.concatenate(
        [jnp.where(lane_head_g == h, q_ins[g], jnp.zeros_like(q_ins[g])) for h in range(H)],
        axis=0) for g in groups]
    atts = [jnp.where(att_keep, _dot_nt(q_stacks[g], k_ins[g]), 0.0).astype(BF16)
            for g in groups]
    for g in groups:
        for h in range(H):
            k_t = k_st_ts[g][h * GLA_DK:(h + 1) * GLA_DK, :]
            lhs = jnp.where(chunk_rows, jnp.concatenate([k_t] * cpg, axis=0), 0.0).astype(BF16)
            d_h = _dot(lhs, v_ref[rows[g], h * GLA_DV:(h + 1) * GLA_DV])
            for c in range(cpg):
                dstate_ref[g * cpg + c, h * GLA_DK:(h + 1) * GLA_DK, :] = (
                    d_h[c * GLA_DK:(c + 1) * GLA_DK])
        for c in range(cpg):
            decay_ref[g * cpg + c] = jnp.exp(
                jnp.transpose(jnp.broadcast_to(tots[g][c], (GLA_DV, GLA_QK))))
    o_intras = [jnp.concatenate(
        [_dot(atts[g][h * G:(h + 1) * G], v_ref[rows[g], h * GLA_DV:(h + 1) * GLA_DV])
         for h in range(H)], axis=1) for g in groups]
    q_in_chunks = [q_ins[g][c * C:(c + 1) * C] for g in groups for c in range(cpg)]
    o_intra_chunks = [o_intras[g][c * C:(c + 1) * C] for g in groups for c in range(cpg)]

    state = state_ref[...]
    for i in range(n_chunks):
        ch = jnp.where(is_fwd, i, n_chunks - 1 - i)
        states_ref[ch] = state
        state = decay_ref[ch] * state + dstate_ref[ch]
    state_ref[...] = state

    for ch in range(n_chunks):
        q_stack = jnp.concatenate(
            [jnp.where(lane_head_c == h, q_in_chunks[ch], jnp.zeros_like(q_in_chunks[ch]))
             for h in range(H)], axis=0)
        inter = _dot(q_stack, states_ref[ch].astype(BF16))
        oblk_ref[ch * C:(ch + 1) * C, :] = o_intra_chunks[ch] + jnp.concatenate(
            [inter[h * C:(h + 1) * C] for h in range(H)], axis=1)

    blk = jnp.where(is_fwd, j, n_blocks - 1 - j)
    row0 = pl.multiple_of(blk * tb, tb)

    @pl.when(is_fwd)
    def _():
        ofwd_ref[pl.ds(row0, tb), :] = oblk_ref[...]

    @pl.when(jnp.logical_not(is_fwd))
    def _():
        ng = ng_ref[...]
        for r in range(0, tb, C):
            o = ofwd_ref[pl.ds(pl.multiple_of(row0 + r, C), C), :] + oblk_ref[r:r + C, :]
            normed = jnp.concatenate(
                [_rms_norm(o[:, h * GLA_DV:(h + 1) * GLA_DV], ng) for h in range(H)], axis=1)
            o_ref[r:r + C, :] = (normed * _silu(r_ref[r:r + C, :].astype(F32))).astype(BF16)


def _gla(gq, gk, gv, gr, la, norm_g, batch, seq, tb):
    nb = seq // tb

    def blk(ph, j):
        return jnp.where(ph == 0, j, nb - 1 - j)

    def row(width):
        return pl.BlockSpec((tb, width), lambda b, ph, j: (b * nb + blk(ph, j), 0))

    out_spec = pl.BlockSpec((tb, GLA_VW), lambda b, ph, j: (b * nb + nb - 1 - ph * j, 0))
    kern = functools.partial(_gla_kernel, n_blocks=nb, tb=tb)
    return pl.pallas_call(
        kern,
        grid=(batch, 2, nb),
        in_specs=[row(GLA_QK), row(GLA_QK), row(GLA_VW), row(GLA_VW),
                  pl.BlockSpec((1, tb, GLA_QK), lambda b, ph, j: (ph, b * nb + blk(ph, j), 0)),
                  _resident(norm_g.shape)],
        out_specs=out_spec,
        out_shape=jax.ShapeDtypeStruct((batch * seq, GLA_VW), BF16),
        scratch_shapes=[pltpu.VMEM((GLA_QK, GLA_DV), F32),
                        pltpu.VMEM((seq, GLA_VW), F32),
                        pltpu.VMEM((tb, GLA_VW), F32),
                        pltpu.VMEM((tb // GLA_CHUNK, GLA_QK, GLA_DV), F32),
                        pltpu.VMEM((tb // GLA_CHUNK, GLA_QK, GLA_DV), F32),
                        pltpu.VMEM((tb // GLA_CHUNK, GLA_QK, GLA_DV), F32)],
        compiler_params=_params(3),
        name="gla",
    )(gq, gk, gv, gr, la, norm_g)


def _outproj_ln_kernel(x_ref, ot_ref, og_ref, wa_ref, wb_ref, g_ref, b_ref, o_ref):
    y = _dot_tn(ot_ref[0], wa_ref[...]) + _dot(og_ref[...], wb_ref[...])
    _layer_norm_rows(o_ref, lambda r: DN_ALPHA * x_ref[r, :] + y[r, :], g_ref[...], b_ref[...])


def _outproj_ln(x2, ot, og, wa, wb, g, b, seq, tm):
    n = x2.shape[0]
    nt = seq // tm
    row = lambda width: pl.BlockSpec((tm, width), lambda i: (i, 0))
    return pl.pallas_call(
        _outproj_ln_kernel,
        grid=(n // tm,),
        in_specs=[row(D_MODEL),
                  pl.BlockSpec((1, ot.shape[1], tm), lambda i: (i // nt, 0, i % nt)),
                  row(GLA_VW),
                  _resident(wa.shape), _resident(wb.shape), _resident(g.shape), _resident(b.shape)],
        out_specs=row(D_MODEL),
        out_shape=jax.ShapeDtypeStruct((n, D_MODEL), F32),
        compiler_params=_params(1),
        name="outproj_ln",
    )(x2, ot, og, wa, wb, g, b)


def _sgu_ln_kernel(x_ref, win_ref, lng_ref, lnb_ref, ws_ref, bs_ref, wout_ref, g_ref, b_ref,
                   o_ref, v_ref, vn_ref, p_ref, z_ref):
    i = pl.program_id(0)

    @pl.when(i == 0)
    def _():
        z_ref[1] = jnp.zeros(z_ref.shape[1:], F32)

    prev = (i + 1) % 2
    x = x_ref[...]
    xb = x.astype(BF16)
    tm = x.shape[0]
    n_col = SGU_HALF // SGU_COLS
    ln_rows = tm // n_col
    for jc in range(n_col):
        lo = SGU_HALF + jc * SGU_COLS
        v_ref[:, jc * SGU_COLS:(jc + 1) * SGU_COLS] = _gelu(_dot(xb, win_ref[:, lo:lo + SGU_COLS]))
    def u_chunk(jc):
        hu = _dot(xb, win_ref[:, jc * SGU_COLS:(jc + 1) * SGU_COLS])
        r0 = jc * ln_rows
        _layer_norm_rows(o_ref.at[r0:r0 + ln_rows, :],
                         lambda r: z_ref[prev, r0 + r.start:r0 + r.stop, :],
                         g_ref[...], b_ref[...])
        return _gelu(hu + _zero_after_store(o_ref, i))

    u_next = u_chunk(0)
    lng = lng_ref[...]
    lnb = lnb_ref[...]
    for r in range(0, tm, SGU_LN_ROWS):
        vn_ref[r:r + SGU_LN_ROWS, :] = _layer_norm(v_ref[r:r + SGU_LN_ROWS, :], lng, lnb).astype(BF16)

    for jc in range(n_col):
        u = u_next
        if jc + 1 < n_col:
            u_next = u_chunk(jc + 1)
        for gg in range(SGU_COLS // SGU_GROUP_DIM):
            grp = jc * (SGU_COLS // SGU_GROUP_DIM) + gg
            c0 = grp * SGU_GROUP_DIM
            bias = jnp.concatenate([bs_ref[grp]] * (SGU_GROUP_DIM // LANE), axis=1)
            for n in range(tm // SGU_CHUNK):
                t0 = n * SGU_CHUNK
                mixed = _dot(ws_ref[grp], vn_ref[t0:t0 + SGU_CHUNK, c0:c0 + SGU_GROUP_DIM]) + bias
                ug = u[t0:t0 + SGU_CHUNK, gg * SGU_GROUP_DIM:(gg + 1) * SGU_GROUP_DIM]
                p_ref[t0:t0 + SGU_CHUNK, c0:c0 + SGU_GROUP_DIM] = (ug * mixed).astype(BF16)
    y = _dot(p_ref[...], wout_ref[...])
    z_ref[i % 2] = DN_ALPHA * x + y


def _sgu_ln(x2, win, lng, lnb, ws, bs, wout, g, b, tm):
    n = x2.shape[0]
    nt = n // tm
    row = pl.BlockSpec((tm, D_MODEL), lambda i: (jnp.minimum(i, nt - 1), 0))
    return pl.pallas_call(
        _sgu_ln_kernel,
        grid=(nt + 1,),
        in_specs=[row, _resident(win.shape), _resident(lng.shape), _resident(lnb.shape),
                  _resident(ws.shape), _resident(bs.shape), _resident(wout.shape),
                  _resident(g.shape), _resident(b.shape)],
        out_specs=pl.BlockSpec((tm, D_MODEL), lambda i: (jnp.maximum(i - 1, 0), 0)),
        out_shape=jax.ShapeDtypeStruct((n, D_MODEL), F32),
        scratch_shapes=[pltpu.VMEM((tm, SGU_HALF), F32),
                        pltpu.VMEM((tm, SGU_HALF), BF16),
                        pltpu.VMEM((tm, SGU_HALF), BF16),
                        pltpu.VMEM((2, tm, D_MODEL), F32)],
        compiler_params=_params(1),
        name="sgu_ln",
    )(x2, win, lng, lnb, ws, bs, wout, g, b)


def _row(v):
    return v.reshape(1, -1).astype(F32)


def _pack_ffn(w_gu, w_down):
    return w_gu.astype(BF16), w_down.astype(BF16)


def _rope_tab(seq):
    inv_freq = 1.0 / (ROPE_BASE ** (jnp.arange(0, MLA_ROPE, 2, dtype=F32) / MLA_ROPE))
    ang = jnp.arange(seq, dtype=F32)[:, None] * inv_freq[None, :]
    cos, sin = jnp.cos(ang), jnp.sin(ang)
    pad_lo = jnp.zeros((seq, MLA_NOPE), F32)
    pad_hi = jnp.zeros((seq, MLA_HEAD_PAD - MLA_NOPE - MLA_ROPE), F32)
    k_dir = jnp.concatenate([pad_lo, cos, cos, pad_hi], axis=1)
    k_swp = jnp.concatenate([pad_lo, -sin, sin, pad_hi], axis=1)
    q_scale = (MLA_NOPE + MLA_ROPE) ** -0.5 * math.log2(math.e)
    q_dir = jnp.concatenate([jnp.ones((seq, MLA_NOPE), F32), cos, cos, pad_hi], axis=1) * q_scale
    q_swp = k_swp * q_scale
    return jnp.stack([q_dir, q_swp, k_dir, k_swp])


def _pack_even(w_in, q_norm, w_uq, kv_norm, w_ukv, wg_f, bg_f, wg_b, bg_b, seqs):
    o = 0
    parts = {}
    for name, width in (("cq", MLA_Q_RANK), ("ckv", MLA_KV_RANK), ("kr", MLA_ROPE),
                        ("q", GLA_QK), ("k", GLA_QK), ("v", GLA_VW), ("r", GLA_VW),
                        ("zf", GLA_GATE_RANK), ("zb", GLA_GATE_RANK)):
        parts[name] = w_in[:, o:o + width]
        o += width
    zcol = lambda n: jnp.zeros((D_MODEL, n), F32)
    m1 = jnp.concatenate([parts["zf"], parts["zb"], zcol(MLA_NOPE - 2 * GLA_GATE_RANK),
                          parts["kr"], zcol(MLA_HEAD_PAD - MLA_NOPE - MLA_ROPE)], axis=1)
    win = jnp.concatenate([parts["cq"], parts["ckv"], parts["q"] * (GLA_DK ** -0.5), parts["k"],
                           parts["v"], parts["r"], m1], axis=1).astype(BF16)

    qk = MLA_NOPE + MLA_ROPE
    uq = w_uq.reshape(MLA_Q_RANK, MLA_HEADS, qk)
    pad = MLA_HEAD_PAD - qk
    wqa = jnp.pad(uq, ((0, 0), (0, 0), (0, pad))).reshape(MLA_Q_RANK, -1).astype(BF16)

    ukv = w_ukv.reshape(MLA_KV_RANK, MLA_HEADS, MLA_NOPE + MLA_V)
    wka = jnp.pad(ukv[:, :, :MLA_NOPE], ((0, 0), (0, 0), (0, MLA_HEAD_PAD - MLA_NOPE)))
    wka = wka.reshape(MLA_KV_RANK, -1).astype(BF16)
    wvt = ukv[:, :, MLA_NOPE:].reshape(MLA_KV_RANK, -1).T.astype(BF16)

    gpad = lambda w, at: jnp.pad(w, ((at, LANE - at - GLA_GATE_RANK), (0, 0)))
    wg = jnp.stack([gpad(wg_f, 0), gpad(wg_b, GLA_GATE_RANK)]).astype(BF16)
    bg = jnp.stack([_row(bg_f), _row(bg_b)])
    return dict(win=win, q_norm=_row(q_norm), wqa=wqa, kv_norm=_row(kv_norm),
                wka=wka, wvt=wvt, wg=wg, bg=bg,
                rope_tab={s: _rope_tab(s) for s in seqs})


def _tile(n, pref):
    return pref if n % pref == 0 else n


def kernel(x_prompt, x_sample, l0_ffa_w_gu, l0_ffa_w_down, l0_ln1_g, l0_ln1_b, l0_w_in, l0_mla_q_norm, l0_mla_w_uq, l0_mla_kv_norm, l0_mla_w_ukv, l0_gla_w_gate_f, l0_gla_b_gate_f, l0_gla_w_gate_b, l0_gla_b_gate_b, l0_gla_norm, l0_w_out, l0_ln2_g, l0_ln2_b, l0_ffb_w_gu, l0_ffb_w_down, l0_ln3_g, l0_ln3_b, l1_ffa_w_gu, l1_ffa_w_down, l1_ln1_g, l1_ln1_b, l1_sgu_w_in, l1_sgu_ln_g, l1_sgu_ln_b, l1_sgu_w_s, l1_sgu_b_s, l1_sgu_w_out, l1_ln2_g, l1_ln2_b, l1_ffb_w_gu, l1_ffb_w_down, l1_ln3_g, l1_ln3_b):
    seqs = sorted({x_prompt.shape[1], x_sample.shape[1]})
    ffn = [
        _pack_ffn(l0_ffa_w_gu, l0_ffa_w_down) + (_row(l0_ln1_g), _row(l0_ln1_b)),
        _pack_ffn(l0_ffb_w_gu, l0_ffb_w_down) + (_row(l0_ln3_g), _row(l0_ln3_b)),
        _pack_ffn(l1_ffa_w_gu, l1_ffa_w_down) + (_row(l1_ln1_g), _row(l1_ln1_b)),
        _pack_ffn(l1_ffb_w_gu, l1_ffb_w_down) + (_row(l1_ln3_g), _row(l1_ln3_b)),
    ]
    even = _pack_even(l0_w_in, l0_mla_q_norm, l0_mla_w_uq, l0_mla_kv_norm, l0_mla_w_ukv,
                      l0_gla_w_gate_f, l0_gla_b_gate_f, l0_gla_w_gate_b, l0_gla_b_gate_b, seqs)
    hv = MLA_HEADS * MLA_V
    w_out_a = l0_w_out[:hv].astype(BF16)
    w_out_b = l0_w_out[hv:].astype(BF16)
    gla_norm = _row(l0_gla_norm)
    sgu_bs = jnp.broadcast_to(l1_sgu_b_s[:, :, None], (SGU_GROUPS, SGU_CHUNK, LANE)).astype(F32)
    sgu = (l1_sgu_w_in.astype(BF16), _row(l1_sgu_ln_g), _row(l1_sgu_ln_b),
           l1_sgu_w_s.astype(BF16), sgu_bs, l1_sgu_w_out.astype(BF16),
           _row(l1_ln2_g), _row(l1_ln2_b))

    def trunk(x):
        batch, seq, _ = x.shape
        tm_wide = _tile(seq, 1024)
        tm_half = _tile(seq, 512)
        x2 = x.reshape(batch * seq, D_MODEL)
        x2 = _ffn_ln(x2, *ffn[0], tm_wide)
        q, k, vt, gq, gk, gv, gr, la = _proj_ab(x2, batch, seq, even, tm_wide)
        hp = MLA_HEADS * MLA_HEAD_PAD
        ot = _mla_attn(q.reshape(batch, seq, hp), k.reshape(batch, seq, hp), vt, tm_half)
        og = _gla(gq, gk, gv, gr, la, gla_norm, batch, seq, tm_wide)
        x2 = _outproj_ln(x2, ot, og, w_out_a, w_out_b, _row(l0_ln2_g), _row(l0_ln2_b), seq, tm_wide)
        x2 = _ffn_ln(x2, *ffn[1], tm_wide)
        x2 = _ffn_ln(x2, *ffn[2], tm_wide)
        x2 = _sgu_ln(x2, *sgu, tm_half)
        x2 = _ffn_ln(x2, *ffn[3], tm_wide)
        return x2.reshape(batch, seq, D_MODEL)

    return (trunk(x_prompt), trunk(x_sample))
```
